```python
import math
import jax, jax.numpy as jnp
from jax import lax
import numpy as np

D_MODEL = 1024
BATCH = 8
SEQ = 4096
DEPTH = 4

HEAD_DIM = 64
GMLP_WIDTH = D_MODEL // 4
GMLP_GROUPS = GMLP_WIDTH // HEAD_DIM
GMLP_CHUNK = 128
DSA_WIDTH = 3 * D_MODEL // 8
DSA_HEADS = DSA_WIDTH // HEAD_DIM
DSA_PATTERNS = ((128, 1), (512, 4), (2048, 16))
DSA_BLOCK = 64
GDN_WIDTH = D_MODEL - GMLP_WIDTH - DSA_WIDTH
GDN_HEADS = GDN_WIDTH // HEAD_DIM
GDN_CONV = 5
GDN_CHUNK = 64
ROT_DIM = HEAD_DIM // 4
ROPE_THETA = 500000.0
N_EXPERTS = 16
D_EXPERT = D_MODEL
EC_CAPACITY = 2
PLE_DIM = 256
NORM_EPS = 1e-6
MASK_VALUE = -1e30
IN_SPLITS = (GMLP_WIDTH, GMLP_WIDTH, DSA_WIDTH, DSA_WIDTH, DSA_WIDTH,
             3 * GDN_WIDTH, GDN_WIDTH, 2 * GDN_HEADS, 2 * GDN_HEADS)
IN_WIDTH = sum(IN_SPLITS)
MIX_WIDTH = GMLP_WIDTH + DSA_WIDTH + GDN_WIDTH

kernel_name = 'hybrid_bidir_encoder_block'

F32 = jnp.float32


def rmsnorm(x, g):
    xf = x.astype(F32)
    y = xf * lax.rsqrt(jnp.mean(xf * xf, axis=-1, keepdims=True) + NORM_EPS) * g.astype(F32)
    return y.astype(x.dtype)


def l2norm(x):
    return x * lax.rsqrt(jnp.sum(x * x, axis=-1, keepdims=True) + NORM_EPS)


def partial_rotary(t, positions):
    half = ROT_DIM // 2
    inv_freq = ROPE_THETA ** (-jnp.arange(half, dtype=F32) * 2.0 / ROT_DIM)
    ang = positions.astype(F32)[..., None] * inv_freq
    cos = jnp.cos(ang)[:, :, None, :]
    sin = jnp.sin(ang)[:, :, None, :]
    tf = t.astype(F32)
    x1, x2, rest = tf[..., :half], tf[..., half:ROT_DIM], tf[..., ROT_DIM:]
    out = jnp.concatenate([x1 * cos - x2 * sin, x2 * cos + x1 * sin, rest], axis=-1)
    return out.astype(t.dtype)


def gmlp_spatial_gating(u, v, ln_g, ln_b, w_s, b_s):
    B_, S, _ = u.shape
    u = jax.nn.gelu(u).reshape(B_, S, GMLP_GROUPS, HEAD_DIM)
    vf = jax.nn.gelu(v).astype(F32).reshape(B_, S, GMLP_GROUPS, HEAD_DIM)
    mu = jnp.mean(vf, axis=-1, keepdims=True)
    var = jnp.mean(jnp.square(vf - mu), axis=-1, keepdims=True)
    vn = ((vf - mu) * lax.rsqrt(var + NORM_EPS) * ln_g.astype(F32) + ln_b.astype(F32)).astype(u.dtype)
    vc = vn.reshape(B_, S // GMLP_CHUNK, GMLP_CHUNK, GMLP_GROUPS, HEAD_DIM)
    mixed = jnp.einsum('gij,bcjgd->bcigd', w_s, vc) + b_s.T[None, None, :, :, None]
    return (u * mixed.reshape(B_, S, GMLP_GROUPS, HEAD_DIM)).reshape(B_, S, GMLP_WIDTH)


def dilated_branch(q, k, v, window, dil):
    B_, S, H, Dh = q.shape
    steps = window // (2 * dil)
    L = S // dil
    nb = -(-L // DSA_BLOCK)
    Lp = nb * DSA_BLOCK

    def to_sub(t):
        return t.reshape(B_, L, dil, H, Dh).transpose(0, 2, 3, 1, 4)

    def neighbours(t):
        tp = jnp.pad(t, ((0, 0), (0, 0), (0, 0), (DSA_BLOCK, Lp - L + DSA_BLOCK), (0, 0)))
        tp = tp.reshape(B_, dil, H, nb + 2, DSA_BLOCK, Dh)
        return jnp.concatenate([tp[:, :, :, :-2], tp[:, :, :, 1:-1], tp[:, :, :, 2:]], axis=4)

    qs = to_sub(q)
    qb = jnp.pad(qs, ((0, 0), (0, 0), (0, 0), (0, Lp - L), (0, 0))).reshape(B_, dil, H, nb, DSA_BLOCK, Dh)
    kb, vb = neighbours(to_sub(k)), neighbours(to_sub(v))
    qi = jnp.arange(Lp).reshape(nb, DSA_BLOCK)
    kj = jnp.arange(nb)[:, None] * DSA_BLOCK - DSA_BLOCK + jnp.arange(3 * DSA_BLOCK)[None, :]
    rel = kj[:, None, :] - qi[:, :, None]
    mask = (jnp.abs(rel) <= steps) & (kj[:, None, :] >= 0) & (kj[:, None, :] < L)
    s = jnp.einsum('bxhntc,bxhnsc->bxhnts', qb, kb).astype(F32) * (Dh ** -0.5)
    s = jnp.where(mask, s, MASK_VALUE)
    lse = jax.nn.logsumexp(s, axis=-1)
    pr = jnp.exp(s - lse[..., None]).astype(v.dtype)
    o = jnp.einsum('bxhnts,bxhnsc->bxhntc', pr, vb)
    o = o.reshape(B_, dil, H, Lp, Dh)[:, :, :, :L].transpose(0, 3, 1, 2, 4).reshape(B_, S, H, Dh)
    lse = lse.reshape(B_, dil, H, Lp)[..., :L].transpose(0, 3, 1, 2).reshape(B_, S, H)
    return o, lse


def dilated_attention(q, k, v, positions, q_norm_g, k_norm_g):
    B_, S, _ = q.shape
    heads = lambda t: t.reshape(B_, S, DSA_HEADS, HEAD_DIM)
    q = partial_rotary(rmsnorm(heads(q), q_norm_g), positions)
    k = partial_rotary(rmsnorm(heads(k), k_norm_g), positions)
    v = heads(v)
    outs, lses = [], []
    for window, dil in DSA_PATTERNS:
        o, l = dilated_branch(q, k, v, window, dil)
        outs.append(o)
        lses.append(l)
    wts = jax.nn.softmax(jnp.stack(lses), axis=0).astype(v.dtype)
    o = jnp.einsum('pbsh,pbshd->bshd', wts, jnp.stack(outs))
    return o.reshape(B_, S, DSA_WIDTH)


def centred_depthwise_conv(x, w):
    K = w.shape[0]
    return lax.conv_general_dilated(x, w[:, None, :].astype(x.dtype), window_strides=(1,),
                                    padding=[(K // 2, K // 2)],
                                    dimension_numbers=('NWC', 'WIO', 'NWC'),
                                    feature_group_count=x.shape[-1])


def gated_delta_chunked(q, k, v, g, beta):
    B_, H, S, Dk = k.shape
    Dv = v.shape[-1]
    C = GDN_CHUNK
    N = S // C
    chunk = lambda t: t.reshape((B_, H, N, C) + t.shape[3:])
    qc, kc, vc = chunk(q * (Dk ** -0.5)), chunk(k), chunk(v)
    gc = jnp.cumsum(chunk(g), axis=-1)
    bc = chunk(beta)
    incl = jnp.tril(jnp.ones((C, C), bool))
    strict = jnp.tril(jnp.ones((C, C), bool), -1)
    decay = jnp.exp(jnp.where(incl, gc[..., :, None] - gc[..., None, :], -jnp.inf))
    k_beta = kc * bc[..., None]
    lower = jnp.where(strict, jnp.einsum('bhntd,bhnsd->bhnts', k_beta, kc) * decay, 0.0)
    eye = jnp.eye(C, dtype=F32)
    t_inv = lax.linalg.triangular_solve(lower + eye, jnp.broadcast_to(eye, lower.shape),
                                        left_side=True, lower=True, unit_diagonal=True)
    u = t_inv @ (vc * bc[..., None])
    w = t_inv @ (k_beta * jnp.exp(gc)[..., None])
    qk = jnp.where(incl, jnp.einsum('bhntd,bhnsd->bhnts', qc, kc) * decay, 0.0)

    def step(state, inp):
        q_n, k_n, u_n, w_n, g_n, qk_n = inp
        v_new = u_n - w_n @ state
        o_n = (q_n * jnp.exp(g_n)[..., None]) @ state + qk_n @ v_new
        g_last = g_n[..., -1:]
        state = state * jnp.exp(g_last)[..., None] + jnp.einsum(
            'bhtd,bhte->bhde', k_n * jnp.exp(g_last - g_n)[..., None], v_new)
        return state, o_n

    xs = tuple(jnp.moveaxis(t, 2, 0) for t in (qc, kc, u, w, gc, qk))
    _, o = lax.scan(step, jnp.zeros((B_, H, Dk, Dv), F32), xs)
    return jnp.moveaxis(o, 0, 2).reshape(B_, H, S, Dv)


def gated_deltanet(qkv, gate, a, b, conv_w, a_log, dt_bias, o_norm_g):
    B_, S, _ = qkv.shape
    out_dtype = qkv.dtype
    qkv = jax.nn.silu(centred_depthwise_conv(qkv, conv_w))
    q, k, v = jnp.split(qkv, 3, axis=-1)
    heads = lambda t: t.reshape(B_, S, GDN_HEADS, HEAD_DIM).transpose(0, 2, 1, 3).astype(F32)
    q, k, v = l2norm(heads(q)), l2norm(heads(k)), heads(v)
    a = a.astype(F32).reshape(B_, S, 2, GDN_HEADS).transpose(2, 0, 3, 1)
    b = b.astype(F32).reshape(B_, S, 2, GDN_HEADS).transpose(2, 0, 3, 1)
    g = -jnp.exp(a_log.astype(F32))[:, None, :, None] * jax.nn.softplus(a + dt_bias.astype(F32)[:, None, :, None])
    beta = jax.nn.sigmoid(b)
    o_f = gated_delta_chunked(q, k, v, g[0], beta[0])
    flip = lambda t: jnp.flip(t, axis=2)
    o_b = flip(gated_delta_chunked(flip(q), flip(k), flip(v), flip(g[1]), flip(beta[1])))
    o = (o_f + o_b).transpose(0, 2, 1, 3)
    o = rmsnorm(o, o_norm_g) * jax.nn.silu(gate.astype(F32).reshape(B_, S, GDN_HEADS, HEAD_DIM))
    return o.reshape(B_, S, GDN_WIDTH).astype(out_dtype)


def expert_choice_ffn(h, w_router, w_gate, w_up, w_down):
    B_, S, D = h.shape
    cap = EC_CAPACITY * S // N_EXPERTS
    logits = jnp.einsum('bsd,de->bse', h, w_router).astype(F32)
    aff = jax.nn.softmax(logits, axis=-1)
    gate, idx = lax.top_k(aff.transpose(0, 2, 1), cap)
    xe = jax.vmap(lambda hb, ib: hb[ib])(h, idx)
    hid = jax.nn.silu(jnp.einsum('becd,edf->becf', xe, w_gate)) * jnp.einsum('becd,edf->becf', xe, w_up)
    ye = jnp.einsum('becf,efd->becd', hid, w_down) * gate[..., None].astype(h.dtype)
    flat = (idx + (jnp.arange(B_, dtype=jnp.int32) * S)[:, None, None]).reshape(-1)
    y = jnp.zeros((B_ * S, D), h.dtype).at[flat].add(ye.reshape(-1, D))
    return y.reshape(B_, S, D)


def setup_inputs(seed: int = 0) -> dict:
    key = jax.random.key(seed)
    ks = jax.random.split(key, 26)

    def nrm(k, shape, fan_in):
        return jax.random.normal(k, shape, F32) * (fan_in ** -0.5)

    def gain(k, shape):
        return 1.0 + 0.02 * jax.random.normal(k, shape, F32)

    x = jax.random.normal(ks[0], (BATCH, SEQ, D_MODEL), F32)
    p = jax.random.normal(ks[1], (DEPTH, BATCH, SEQ, PLE_DIM), F32)
    positions = jnp.arange(SEQ, dtype=jnp.int32)[None, :] + jax.random.randint(ks[2], (BATCH, 1), 0, 1024, dtype=jnp.int32)
    dt = jnp.exp(jax.random.uniform(ks[13], (DEPTH, 2, GDN_HEADS), F32, math.log(1e-3), math.log(1e-1)))
    return {
        'x': x,
        'p': p,
        'positions': positions.astype(jnp.int32),
        'g_mix': gain(ks[3], (DEPTH, D_MODEL)),
        'w_in': nrm(ks[4], (DEPTH, D_MODEL, IN_WIDTH), D_MODEL),
        'ln_v_g': gain(ks[5], (DEPTH, GMLP_GROUPS, HEAD_DIM)),
        'ln_v_b': 0.02 * jax.random.normal(ks[6], (DEPTH, GMLP_GROUPS, HEAD_DIM), F32),
        'w_s': nrm(ks[7], (DEPTH, GMLP_GROUPS, GMLP_CHUNK, GMLP_CHUNK), GMLP_CHUNK),
        'b_s': 1.0 + 0.01 * jax.random.normal(ks[8], (DEPTH, GMLP_GROUPS, GMLP_CHUNK), F32),
        'q_norm_g': gain(ks[9], (DEPTH, HEAD_DIM)),
        'k_norm_g': gain(ks[10], (DEPTH, HEAD_DIM)),
        'conv_w': nrm(ks[11], (DEPTH, GDN_CONV, 3 * GDN_WIDTH), GDN_CONV),
        'a_log': jnp.log(jax.random.uniform(ks[12], (DEPTH, 2, GDN_HEADS), F32, 1.0, 16.0)),
        'dt_bias': dt + jnp.log(-jnp.expm1(-dt)),
        'o_norm_g': gain(ks[14], (DEPTH, HEAD_DIM)),
        'w_out': nrm(ks[15], (DEPTH, MIX_WIDTH, D_MODEL), MIX_WIDTH),
        'g_ffn': gain(ks[16], (DEPTH, D_MODEL)),
        'w_router': nrm(ks[17], (DEPTH, D_MODEL, N_EXPERTS), D_MODEL),
        'w_e_gate': nrm(ks[18], (DEPTH, N_EXPERTS, D_MODEL, D_EXPERT), D_MODEL),
        'w_e_up': nrm(ks[19], (DEPTH, N_EXPERTS, D_MODEL, D_EXPERT), D_MODEL),
        'w_e_down': nrm(ks[20], (DEPTH, N_EXPERTS, D_EXPERT, D_MODEL), D_EXPERT),
        'w_ple': nrm(ks[21], (DEPTH, PLE_DIM, D_MODEL), PLE_DIM),
        'g_ple': gain(ks[22], (DEPTH, D_MODEL)),
        'g_ple_gate': gain(ks[23], (DEPTH, D_MODEL)),
        'w_ple_gate': nrm(ks[24], (DEPTH, D_MODEL, D_MODEL), D_MODEL),
    }


def reference(x, p, positions, g_mix, w_in, ln_v_g, ln_v_b, w_s, b_s, q_norm_g, k_norm_g,
              conv_w, a_log, dt_bias, o_norm_g, w_out, g_ffn, w_router, w_e_gate, w_e_up,
              w_e_down, w_ple, g_ple, g_ple_gate, w_ple_gate):
    split_points = np.cumsum(IN_SPLITS)[:-1].tolist()
    for i in range(DEPTH):
        xn = rmsnorm(x, g_mix[i])
        proj = jnp.einsum('bsd,dn->bsn', xn, w_in[i])
        a_u, a_v, b_q, b_k, b_v, c_qkv, c_gate, c_a, c_b = jnp.split(proj, split_points, axis=-1)
        y_a = gmlp_spatial_gating(a_u, a_v, ln_v_g[i], ln_v_b[i], w_s[i], b_s[i])
        y_b = dilated_attention(b_q, b_k, b_v, positions, q_norm_g[i], k_norm_g[i])
        y_c = gated_deltanet(c_qkv, c_gate, c_a, c_b, conv_w[i], a_log[i], dt_bias[i], o_norm_g[i])
        x = x + jnp.einsum('bsm,md->bsd', jnp.concatenate([y_a, y_b, y_c], axis=-1), w_out[i])
        x = x + expert_choice_ffn(rmsnorm(x, g_ffn[i]), w_router[i], w_e_gate[i], w_e_up[i], w_e_down[i])
        e = rmsnorm(jnp.einsum('bsq,qd->bsd', p[i], w_ple[i]), g_ple[i])
        gate = jax.nn.sigmoid(jnp.einsum('bsd,de->bse', rmsnorm(x, g_ple_gate[i]), w_ple_gate[i]))
        x = x + e * gate
    return x
```

```python
import functools

import jax
import jax.numpy as jnp
from jax import lax
from jax.experimental import pallas as pl
from jax.experimental.pallas import tpu as pltpu

F32 = jnp.float32
BF16 = jnp.bfloat16
I32 = jnp.int32

HEAD_DIM = 64
GMLP_WIDTH = 256
GMLP_GROUPS = 4
GMLP_CHUNK = 128
DSA_WIDTH = 384
DSA_PATTERNS = ((128, 1), (512, 4), (2048, 16))
GDN_WIDTH = 384
GDN_CONV = 5
ROT_DIM = 16
ROPE_THETA = 500000.0
N_EXPERTS = 16
EC_CAPACITY = 2
NORM_EPS = 1e-6
MASK_VALUE = -1e30

LANES = 128
SUBLANES = 8
VMEM_LIMIT_BYTES = 56 * 1024 * 1024

PAIR = 2 * HEAD_DIM
GDN_CHUNK = 64
ATT_HALO = 64
ATT_SUB = 128
OFFS_W = 40


def _cparams(*sem):
    return pltpu.CompilerParams(dimension_semantics=sem, vmem_limit_bytes=VMEM_LIMIT_BYTES)


def _dot(a, b):
    return jnp.dot(a, b, preferred_element_type=F32)


def _dot_nt(a, b):
    return lax.dot_general(a, b, (((1,), (1,)), ((), ())), preferred_element_type=F32)


def _dot_tn(a, b):
    return lax.dot_general(a, b, (((0,), (0,)), ((), ())), preferred_element_type=F32)


def _split2(x):
    hi = x.astype(BF16)
    lo = (x - hi.astype(F32)).astype(BF16)
    return hi, lo


def _split3(x):
    hi = x.astype(BF16)
    r = x - hi.astype(F32)
    mid = r.astype(BF16)
    lo = (r - mid.astype(F32)).astype(BF16)
    return hi, mid, lo


def _group_mean(x, avg):
    hi, lo = _split2(x)
    return _dot(hi, avg) + _dot(lo, avg)


def _rms(x):
    return x * lax.rsqrt(jnp.mean(x * x, axis=-1, keepdims=True) + NORM_EPS)


def _block_avg_matrix(width, group):
    i = jnp.arange(width)
    return jnp.where((i[:, None] // group) == (i[None, :] // group), 1.0 / group, 0.0).astype(BF16)


def _inproj_kernel(x_ref, g_ref, w_ref, *o_refs):
    x = x_ref[...]
    xn = (_rms(x) * g_ref[...]).astype(BF16)
    col = 0
    for o_ref in o_refs:
        n = o_ref.shape[-1]
        o_ref[...] = _dot(xn, w_ref[:, col:col + n])
        col += n


def _inproj(x2d, g, w, widths, tm=512):
    m, d = x2d.shape
    n = w.shape[1]
    return pl.pallas_call(
        _inproj_kernel,
        grid=(m // tm,),
        in_specs=[pl.BlockSpec((tm, d), lambda i: (i, 0)),
                  pl.BlockSpec((1, d), lambda i: (0, 0)),
                  pl.BlockSpec((d, n), lambda i: (0, 0))],
        out_specs=[pl.BlockSpec((tm, wd), lambda i: (i, 0)) for wd in widths],
        out_shape=[jax.ShapeDtypeStruct((m, wd), F32) for wd in widths],
        compiler_params=_cparams("parallel"),
        name="inproj",
    )(x2d, g, w)


def _gmlp_kernel(u_ref, v_ref, lng_ref, lnb_ref, ws_ref, bias_ref, avg_ref, o_ref):
    tm = u_ref.shape[0]
    u = jax.nn.gelu(u_ref[...])
    vf = jax.nn.gelu(v_ref[...])
    avg = avg_ref[...]
    mu = _group_mean(vf, avg)
    dv = vf - mu
    var = _group_mean(dv * dv, avg)
    vn = dv * lax.rsqrt(var + NORM_EPS) * lng_ref[...] + lnb_ref[...]
    grp = lax.broadcasted_iota(I32, (GMLP_CHUNK, GMLP_WIDTH), 1) // HEAD_DIM
    for c in range(tm // GMLP_CHUNK):
        rows = slice(c * GMLP_CHUNK, (c + 1) * GMLP_CHUNK)
        vc = vn[rows].astype(BF16)
        mixed = bias_ref[...]
        for g in range(GMLP_GROUPS):
            mixed = mixed + jnp.where(grp == g, _dot(ws_ref[g], vc), 0.0)
        o_ref[rows, :] = (u[rows] * mixed).astype(BF16)


def _gmlp(pa, ln_g, ln_b, w_s, b_s, tm=512):
    m = pa.shape[0]
    w = GMLP_WIDTH
    bias2d = jnp.repeat(b_s.T, HEAD_DIM, axis=1)
    const = lambda i: (0, 0)
    return pl.pallas_call(
        _gmlp_kernel,
        grid=(m // tm,),
        in_specs=[pl.BlockSpec((tm, w), lambda i: (i, 0)),
                  pl.BlockSpec((tm, w), lambda i: (i, 1)),
                  pl.BlockSpec((1, w), const),
                  pl.BlockSpec((1, w), const),
                  pl.BlockSpec((GMLP_GROUPS, GMLP_CHUNK, GMLP_CHUNK), lambda i: (0, 0, 0)),
                  pl.BlockSpec((GMLP_CHUNK, w), const),
                  pl.BlockSpec((w, w), const)],
        out_specs=pl.BlockSpec((tm, w), lambda i: (i, 0)),
        out_shape=jax.ShapeDtypeStruct((m, w), BF16),
        compiler_params=_cparams("parallel"),
        name="gmlp",
    )(pa, pa, ln_g.reshape(1, w), ln_b.reshape(1, w), w_s.astype(BF16), bias2d,
      _block_avg_matrix(w, HEAD_DIM))


def _rope_table_kernel(pos_ref, invf_ref, sa_ref, sb_ref, cos_ref, sina_ref, sinb_ref):
    ang = pos_ref[...].astype(F32) * invf_ref[...]
    s = jnp.sin(ang)
    cos_ref[...] = jnp.cos(ang)
    sina_ref[...] = s * sa_ref[...]
    sinb_ref[...] = s * sb_ref[...]


def _rope_tables(positions, ts=1024):
    m = positions.size
    half = ROT_DIM // 2
    inv_freq = ROPE_THETA ** (-jnp.arange(half, dtype=F32) * 2.0 / ROT_DIM)
    lane = jnp.arange(PAIR) % HEAD_DIM
    invf = jnp.where(lane < ROT_DIM, inv_freq[lane % half], 0.0).reshape(1, PAIR)
    sa = jnp.where(lane < half, -1.0, 0.0).astype(F32).reshape(1, PAIR)
    sb = jnp.where((lane >= half) & (lane < ROT_DIM), 1.0, 0.0).astype(F32).reshape(1, PAIR)
    const = lambda i: (0, 0)
    row = pl.BlockSpec((ts, PAIR), lambda i: (i, 0))
    return pl.pallas_call(
        _rope_table_kernel,
        grid=(m // ts,),
        in_specs=[pl.BlockSpec((ts, 1), lambda i: (i, 0)),
                  pl.BlockSpec((1, PAIR), const), pl.BlockSpec((1, PAIR), const),
                  pl.BlockSpec((1, PAIR), const)],
        out_specs=[row, row, row],
        out_shape=[jax.ShapeDtypeStruct((m, PAIR), F32)] * 3,
        compiler_params=_cparams("parallel"),
        name="rope_tables",
    )(positions.reshape(m, 1), invf, sa, sb)


def _dsa_prep_kernel(q_ref, k_ref, v_ref, cos_ref, sina_ref, sinb_ref, gq_ref, gk_ref, avg_ref, o_ref):
    w = DSA_WIDTH
    cos, sina, sinb = cos_ref[...], sina_ref[...], sinb_ref[...]
    avg = avg_ref[...]
    half = ROT_DIM // 2

    def norm_rot(t, g, scale, col0):
        t = t * lax.rsqrt(_group_mean(t * t, avg) + NORM_EPS) * g
        for hp in range(w // PAIR):
            tp = t[:, hp * PAIR:(hp + 1) * PAIR]
            rot = tp * cos + pltpu.roll(tp, PAIR - half, 1) * sina + pltpu.roll(tp, half, 1) * sinb
            o_ref[:, col0 + hp * PAIR:col0 + (hp + 1) * PAIR] = (rot * scale).astype(BF16)

    norm_rot(q_ref[...], gq_ref[...], HEAD_DIM ** -0.5, 0)
    norm_rot(k_ref[...], gk_ref[...], 1.0, w)
    o_ref[:, 2 * w:3 * w] = v_ref[...].astype(BF16)


def _dsa_prep(pb, tables, q_norm_g, k_norm_g, tm=512):
    m = pb.shape[0]
    w = DSA_WIDTH
    const = lambda i: (0, 0)
    tab = pl.BlockSpec((tm, PAIR), lambda i: (i, 0))
    gq = jnp.tile(q_norm_g, w // HEAD_DIM).reshape(1, w)
    gk = jnp.tile(k_norm_g, w // HEAD_DIM).reshape(1, w)
    return pl.pallas_call(
        _dsa_prep_kernel,
        grid=(m // tm,),
        in_specs=[pl.BlockSpec((tm, w), lambda i: (i, 0)),
                  pl.BlockSpec((tm, w), lambda i: (i, 1)),
                  pl.BlockSpec((tm, w), lambda i: (i, 2)),
                  tab, tab, tab,
                  pl.BlockSpec((1, w), const), pl.BlockSpec((1, w), const),
                  pl.BlockSpec((w, w), const)],
        out_specs=pl.BlockSpec((tm, 3 * w), lambda i: (i, 0)),
        out_shape=jax.ShapeDtypeStruct((m, 3 * w), BF16),
        compiler_params=_cparams("parallel"),
        name="dsa_prep",
    )(pb, pb, pb, *tables, gq, gk, _block_avg_matrix(w, HEAD_DIM))


def _dsa_kernel(*refs, steps, sub_len, first, last):
    q_ref, kp_ref, k_ref, kn_ref, vp_ref, v_ref, vn_ref = refs[:7]
    refs = refs[7:]
    if not first:
        acc_in_ref, ml_in_ref = refs[:2]
        refs = refs[2:]
    if last:
        o_ref, kbuf, vbuf = refs
    else:
        acc_out_ref, ml_out_ref, kbuf, vbuf = refs
    t = q_ref.shape[0]
    i = pl.program_id(2)
    kbuf[0:ATT_HALO, :] = kp_ref[...]
    kbuf[ATT_HALO:ATT_HALO + t, :] = k_ref[...]
    kbuf[ATT_HALO + t:, :] = kn_ref[...]
    vbuf[0:ATT_HALO, :] = vp_ref[...]
    vbuf[ATT_HALO:ATT_HALO + t, :] = v_ref[...]
    vbuf[ATT_HALO + t:, :] = vn_ref[...]

    nk = ATT_SUB + 2 * ATT_HALO
    lane = lax.broadcasted_iota(I32, (1, PAIR), 1)
    head_mask = (lane < HEAD_DIM, lane >= HEAD_DIM)
    lane_ml = lax.broadcasted_iota(I32, (ATT_SUB, LANES), 1)
    for s in range(t // ATT_SUB):
        rows = slice(s * ATT_SUB, (s + 1) * ATT_SUB)
        base = i * t + s * ATT_SUB
        qi = base + lax.broadcasted_iota(I32, (ATT_SUB, 1), 0)
        kj = base - ATT_HALO + lax.broadcasted_iota(I32, (1, nk), 1)
        valid = (jnp.abs(kj - qi) <= steps) & (kj >= 0) & (kj < sub_len)
        kk = kbuf[s * ATT_SUB:s * ATT_SUB + nk, :]
        vv = vbuf[s * ATT_SUB:s * ATT_SUB + nk, :]
        ml_new = jnp.zeros((ATT_SUB, LANES), F32)
        for hp in range(DSA_WIDTH // PAIR):
            cols = slice(hp * PAIR, (hp + 1) * PAIR)
            qp = q_ref[rows, cols]
            kp = kk[:, cols]
            vp = vv[:, cols]
            acc = jnp.zeros((ATT_SUB, PAIR), F32)
            scale = []
            for hh in range(2):
                h = 2 * hp + hh
                km = jnp.where(head_mask[hh], kp, jnp.zeros_like(kp))
                sc = jnp.where(valid, _dot_nt(qp, km), MASK_VALUE)
                m_new = jnp.max(sc, axis=1, keepdims=True)
                if not first:
                    m_prev = ml_in_ref[rows, h:h + 1]
                    l_prev = ml_in_ref[rows, 8 + h:9 + h]
                    m_new = jnp.maximum(m_prev, m_new)
                    alpha = jnp.exp(m_prev - m_new)
                p = jnp.exp(sc - m_new)
                l_new = jnp.sum(p, axis=1, keepdims=True)
                if not first:
                    l_new = l_new + alpha * l_prev
                    scale.append(alpha)
                vm = jnp.where(head_mask[hh], vp, jnp.zeros_like(vp))
                acc = acc + _dot(p.astype(BF16), vm)
                if last:
                    scale.append(1.0 / l_new)
                else:
                    ml_new = jnp.where(lane_ml == h, m_new, jnp.where(lane_ml == 8 + h, l_new, ml_new))
            if not first:
                prev = acc_in_ref[rows, cols]
                if last:
                    acc = acc + prev * jnp.where(head_mask[0], scale[0], scale[2])
                    acc = acc * jnp.where(head_mask[0], scale[1], scale[3])
                else:
                    acc = acc + prev * jnp.where(head_mask[0], scale[0], scale[1])
            elif last:
                acc = acc * jnp.where(head_mask[0], scale[0], scale[1])
            if last:
                o_ref[rows, cols] = acc.astype(BF16)
            else:
                acc_out_ref[rows, cols] = acc
        if not last:
            ml_out_ref[rows, :] = ml_new


def _dsa_pattern(qkv, state, batch, seq, window, dil, first, last):
    w = DSA_WIDTH
    steps = window // (2 * dil)
    assert steps <= ATT_HALO
    sub_len = seq // dil
    t = min(512, sub_len)
    nb64 = sub_len // ATT_HALO
    r64 = t // ATT_HALO
    grid = (batch, dil, sub_len // t)

    def view(a, width):
        return a.reshape(batch, sub_len, dil * width)

    def main(which):
        return pl.BlockSpec((None, t, w), lambda b, r, i: (b, i, r * 3 + which))

    def prev(which):
        return pl.BlockSpec((None, ATT_HALO, w),
                            lambda b, r, i: (b, jnp.maximum(i * r64 - 1, 0), r * 3 + which))

    def nxt(which):
        return pl.BlockSpec((None, ATT_HALO, w),
                            lambda b, r, i: (b, jnp.minimum((i + 1) * r64, nb64 - 1), r * 3 + which))

    acc_spec = pl.BlockSpec((None, t, w), lambda b, r, i: (b, i, r))
    ml_spec = pl.BlockSpec((None, t, LANES), lambda b, r, i: (b, i, r))
    qv = view(qkv, 3 * w)
    in_specs = [main(0), prev(1), main(1), nxt(1), prev(2), main(2), nxt(2)]
    args = [qv] * 7
    if not first:
        in_specs += [acc_spec, ml_spec]
        args += [view(state[0], w), view(state[1], LANES)]
    if last:
        out_specs = acc_spec
        out_shape = jax.ShapeDtypeStruct((batch, sub_len, dil * w), BF16)
    else:
        out_specs = [acc_spec, ml_spec]
        out_shape = [jax.ShapeDtypeStruct((batch, sub_len, dil * w), F32),
                     jax.ShapeDtypeStruct((batch, sub_len, dil * LANES), F32)]
    out = pl.pallas_call(
        functools.partial(_dsa_kernel, steps=steps, sub_len=sub_len, first=first, last=last),
        grid=grid,
        in_specs=in_specs,
        out_specs=out_specs,
        out_shape=out_shape,
        scratch_shapes=[pltpu.VMEM((t + 2 * ATT_HALO, w), BF16),
                        pltpu.VMEM((t + 2 * ATT_HALO, w), BF16)],
        compiler_params=_cparams("parallel", "parallel", "parallel"),
        name=f"dsa_d{dil}",
    )(*args)
    if last:
        return out.reshape(batch * seq, w)
    return out[0].reshape(batch * seq, w), out[1].reshape(batch * seq, LANES)


def _dsa(qkv, batch, seq):
    state = None
    n = len(DSA_PATTERNS)
    for idx, (window, dil) in enumerate(DSA_PATTERNS):
        state = _dsa_pattern(qkv, state, batch, seq, window, dil, idx == 0, idx == n - 1)
    return state


def _gdn_kernel(xq_ref, xk_ref, xv_ref, gate_ref, ab_ref, cw_ref, alog_ref, dtb_ref, on_ref,
                ones_ref, avg_ref, o_ref, u_s, w_s, qg_s, kd_s, qk_s, el_s, of_s):
    seq = xq_ref.shape[0]
    c = GDN_CHUNK
    nch = seq // c
    halo = SUBLANES
    lane = lax.broadcasted_iota(I32, (1, PAIR), 1)
    head_mask = (lane < HEAD_DIM, lane >= HEAD_DIM)
    ti = lax.broadcasted_iota(I32, (c, c), 0)
    si = lax.broadcasted_iota(I32, (c, c), 1)
    incl = (si <= ti, si >= ti)
    strict = (si < ti, si > ti)
    tri_incl = tuple(jnp.where(m, 1.0, 0.0).astype(BF16) for m in incl)
    bd_ones = ones_ref[...]
    lane8 = lax.broadcasted_iota(I32, (1, LANES), 1)
    neg_a = -jnp.exp(alog_ref[...])
    dtb = dtb_ref[...]

    def conv_silu(x_ref, which, r0, ci):
        main = x_ref[pl.ds(r0, c), :]
        before = x_ref[pl.ds(jnp.maximum(r0 - halo, 0), halo), :]
        after = x_ref[pl.ds(jnp.minimum(r0 + c, seq - halo), halo), :]
        before = jnp.where(ci > 0, before, 0.0)
        after = jnp.where(ci < nch - 1, after, 0.0)
        win = jnp.concatenate([before, main, after], axis=0)
        n = c + 2 * halo
        y = jnp.zeros((c, PAIR), F32)
        for j in range(GDN_CONV):
            shifted = win if j == GDN_CONV // 2 else pltpu.roll(win, (GDN_CONV // 2 - j) % n, 0)
            y = y + shifted[halo:halo + c] * cw_ref[which, j:j + 1, :]
        return y * jax.nn.sigmoid(y)

    def l2n(x):
        hi, lo = _split2(x * x)
        ss = _dot(hi, bd_ones) + _dot(lo, bd_ones)
        return x * lax.rsqrt(ss + NORM_EPS)

    def transform(ci, carry):
        r0 = pl.multiple_of(ci * c, c)
        q = l2n(conv_silu(xq_ref, 0, r0, ci)) * (HEAD_DIM ** -0.5)
        k = l2n(conv_silu(xk_ref, 1, r0, ci))
        v = conv_silu(xv_ref, 2, r0, ci)
        ab = ab_ref[pl.ds(r0, c), :]
        gb = jnp.where(lane8 < 4, neg_a * jax.nn.softplus(ab + dtb), jax.nn.sigmoid(ab))
        g3 = _split3(gb)
        for d in range(2):
            gc = sum(_dot(tri_incl[d], part) for part in g3)
            gct = gc.T
            g_last = gc[c - 1:c, :] if d == 0 else gc[0:1, :]
            e_gc, e_rem, e_last, u_sum, w_sum = [], [], [], 0.0, 0.0
            for hh in range(2):
                col = 2 * d + hh
                gc_c = gc[:, col:col + 1]
                gc_r = gct[col:col + 1, :]
                beta = gb[:, 4 + col:5 + col]
                decay = jnp.where(incl[d], jnp.exp(jnp.minimum(gc_c - gc_r, 0.0)), 0.0)
                km = jnp.where(head_mask[hh], k, 0.0)
                kb = km * beta
                kmb = km.astype(BF16)
                lmat = jnp.where(strict[d], _dot_nt(kb.astype(BF16), kmb) * decay, 0.0)
                qm = jnp.where(head_mask[hh], q, 0.0)
                qk = jnp.where(incl[d], _dot_nt(qm.astype(BF16), kmb) * decay, 0.0)
                qk_s[col, pl.ds(r0, c), :] = qk.astype(BF16)
                eg = jnp.exp(gc_c)
                x = jnp.concatenate([jnp.where(head_mask[hh], v, 0.0) * beta, kb * eg], axis=1)
                powers = [lmat.astype(BF16)]
                for _ in range(5):
                    pw = powers[-1]
                    powers.append(_dot(pw, pw).astype(BF16))
                for pw in powers[:0:-1]:
                    x = x + _dot(pw, x.astype(BF16))
                x = x - _dot(powers[0], x.astype(BF16))
                u_sum = u_sum + x[:, :PAIR]
                w_sum = w_sum + x[:, PAIR:]
                gl = g_last[:, col:col + 1]
                e_gc.append(eg)
                e_rem.append(jnp.exp(gl - gc_c))
                e_last.append(jnp.exp(gl))
            u_s[d, pl.ds(r0, c), :] = u_sum
            w_s[d, pl.ds(r0, c), :] = w_sum.astype(BF16)
            qg_s[d, pl.ds(r0, c), :] = (q * jnp.where(head_mask[0], e_gc[0], e_gc[1])).astype(BF16)
            kd_s[d, pl.ds(r0, c), :] = (k * jnp.where(head_mask[0], e_rem[0], e_rem[1])).astype(BF16)
            el_s[d, pl.ds(ci, 1), :] = jnp.where(head_mask[0], e_last[0], e_last[1])
        return carry

    lax.fori_loop(0, nch, transform, 0)

    li = lax.broadcasted_iota(I32, (PAIR, PAIR), 0) // HEAD_DIM
    lj = lax.broadcasted_iota(I32, (PAIR, PAIR), 1) // HEAD_DIM
    same_head = li == lj

    def scan(i, states):
        new_states = []
        for d in range(2):
            ci = i if d == 0 else nch - 1 - i
            r0 = pl.multiple_of(ci * c, c)
            st = states[d]
            stb = st.astype(BF16)
            v_new = u_s[d, pl.ds(r0, c), :] - _dot(w_s[d, pl.ds(r0, c), :], stb)
            o = _dot(qg_s[d, pl.ds(r0, c), :], stb)
            for hh in range(2):
                vm = jnp.where(head_mask[hh], v_new, 0.0).astype(BF16)
                o = o + _dot(qk_s[2 * d + hh, pl.ds(r0, c), :], vm)
            of_s[d, pl.ds(r0, c), :] = o
            upd = _dot_tn(kd_s[d, pl.ds(r0, c), :], v_new.astype(BF16))
            new_states.append(st * el_s[d, pl.ds(ci, 1), :] + jnp.where(same_head, upd, 0.0))
        return tuple(new_states)

    zero = jnp.zeros((PAIR, PAIR), F32)
    lax.fori_loop(0, nch, scan, (zero, zero))

    avg = avg_ref[...]
    tile = 512

    def finish(ti_, carry):
        r0 = pl.multiple_of(ti_ * tile, tile)
        o = of_s[0, pl.ds(r0, tile), :] + of_s[1, pl.ds(r0, tile), :]
        o = o * lax.rsqrt(_group_mean(o * o, avg) + NORM_EPS) * on_ref[...]
        gate = gate_ref[pl.ds(r0, tile), :]
        o_ref[pl.ds(r0, tile), :] = (o * (gate * jax.nn.sigmoid(gate))).astype(BF16)
        return carry

    lax.fori_loop(0, seq // tile, finish, 0)


def _gdn(pc, batch, seq, conv_w, a_log, dt_bias, o_norm_g):
    w = GDN_WIDTH
    heads = w // HEAD_DIM
    pairs = w // PAIR
    pc3 = pc.reshape(batch, seq, pc.shape[-1])
    ab = pc3[:, :, 4 * w:4 * w + 4 * heads].reshape(batch, seq, 2, 2, pairs, 2)
    ab = ab.transpose(0, 4, 1, 2, 3, 5).reshape(batch, pairs, seq, 8)
    ab = jnp.pad(ab, ((0, 0), (0, 0), (0, 0), (0, LANES - 8)))
    cw = conv_w.reshape(GDN_CONV, 3, w).transpose(1, 0, 2)
    cw = jnp.pad(cw, ((0, 0), (0, SUBLANES - GDN_CONV), (0, 0)))

    def pair_lanes(p):
        x = p.reshape(2, pairs, 2).transpose(1, 0, 2).reshape(pairs, 1, 4)
        return jnp.pad(x, ((0, 0), (0, 0), (0, LANES - 4)))

    nblk = w // PAIR

    def col(offset):
        return pl.BlockSpec((None, seq, PAIR), lambda b, hp: (b, 0, offset + hp))

    nch = seq // GDN_CHUNK
    return pl.pallas_call(
        _gdn_kernel,
        grid=(batch, pairs),
        in_specs=[col(0), col(nblk), col(2 * nblk), col(3 * nblk),
                  pl.BlockSpec((None, None, seq, LANES), lambda b, hp: (b, hp, 0, 0)),
                  pl.BlockSpec((3, SUBLANES, PAIR), lambda b, hp: (0, 0, hp)),
                  pl.BlockSpec((None, 1, LANES), lambda b, hp: (hp, 0, 0)),
                  pl.BlockSpec((None, 1, LANES), lambda b, hp: (hp, 0, 0)),
                  pl.BlockSpec((1, PAIR), lambda b, hp: (0, 0)),
                  pl.BlockSpec((PAIR, PAIR), lambda b, hp: (0, 0)),
                  pl.BlockSpec((PAIR, PAIR), lambda b, hp: (0, 0))],
        out_specs=pl.BlockSpec((None, seq, PAIR), lambda b, hp: (b, 0, hp)),
        out_shape=jax.ShapeDtypeStruct((batch, seq, w), BF16),
        scratch_shapes=[pltpu.VMEM((2, seq, PAIR), F32),
                        pltpu.VMEM((2, seq, PAIR), BF16),
                        pltpu.VMEM((2, seq, PAIR), BF16),
                        pltpu.VMEM((2, seq, PAIR), BF16),
                        pltpu.VMEM((4, seq, GDN_CHUNK), BF16),
                        pltpu.VMEM((2, nch, PAIR), F32),
                        pltpu.VMEM((2, seq, PAIR), F32)],
        compiler_params=_cparams("parallel", "parallel"),
        name="gdn",
    )(pc3, pc3, pc3, pc3, ab, cw, pair_lanes(a_log), pair_lanes(dt_bias),
      jnp.tile(o_norm_g, 2).reshape(1, PAIR),
      (_block_avg_matrix(PAIR, HEAD_DIM) * HEAD_DIM).astype(BF16),
      _block_avg_matrix(PAIR, HEAD_DIM)).reshape(batch * seq, w)


def _outproj_kernel(x_ref, ya_ref, yb_ref, yc_ref, wa_ref, wb_ref, wc_ref, g_ref, x1_ref, h_ref):
    x1 = (x_ref[...] + _dot(ya_ref[...], wa_ref[...]) + _dot(yb_ref[...], wb_ref[...])
          + _dot(yc_ref[...], wc_ref[...]))
    x1_ref[...] = x1
    h_ref[...] = (_rms(x1) * g_ref[...]).astype(BF16)


def _outproj(x2d, ya, yb, yc, w_out, g_ffn, tm=512):
    m, d = x2d.shape
    wa = w_out[:GMLP_WIDTH]
    wb = w_out[GMLP_WIDTH:GMLP_WIDTH + DSA_WIDTH]
    wc = w_out[GMLP_WIDTH + DSA_WIDTH:]
    const = lambda i: (0, 0)
    row = lambda wd: pl.BlockSpec((tm, wd), lambda i: (i, 0))
    return pl.pallas_call(
        _outproj_kernel,
        grid=(m // tm,),
        in_specs=[row(d), row(GMLP_WIDTH), row(DSA_WIDTH), row(GDN_WIDTH),
                  pl.BlockSpec(wa.shape, const), pl.BlockSpec(wb.shape, const),
                  pl.BlockSpec(wc.shape, const), pl.BlockSpec((1, d), const)],
        out_specs=[row(d), row(d)],
        out_shape=[jax.ShapeDtypeStruct((m, d), F32), jax.ShapeDtypeStruct((m, d), BF16)],
        compiler_params=_cparams("parallel"),
        name="outproj",
    )(x2d, ya, yb, yc, wa, wb, wc, g_ffn.reshape(1, d))


def _router_kernel(h_ref, wr_ref, tri_ref, pos_ref, aff_ref, offs_ref, bits_s, *, cap):
    seq = h_ref.shape[0]
    nblk = seq // LANES
    logits = _dot_nt(wr_ref[...], h_ref[...])
    ex = jnp.exp(logits - jnp.max(logits, axis=0, keepdims=True))
    aff = ex / jnp.sum(ex, axis=0, keepdims=True)
    aff_ref[...] = aff
    bits = lax.bitcast_convert_type(aff, I32)
    bits_s[...] = bits

    def bisect(it, prefix):
        cand = prefix | jnp.left_shift(jnp.int32(1), 30 - it)
        cnt = jnp.sum(jnp.where(bits_s[...] >= cand, 1.0, 0.0), axis=1, keepdims=True)
        return jnp.where(cnt >= cap, cand, prefix)

    thr = lax.fori_loop(0, 31, bisect, jnp.zeros((N_EXPERTS, 1), I32))
    n_gt = jnp.sum(jnp.where(bits > thr, 1.0, 0.0), axis=1, keepdims=True)
    need = cap - n_gt
    tri = tri_ref[...]
    lane = lax.broadcasted_iota(I32, (N_EXPERTS, LANES), 1)

    off_eq = jnp.zeros((N_EXPERTS, 1), F32)
    off_sel = jnp.zeros((N_EXPERTS, 1), F32)
    offs = jnp.zeros((N_EXPERTS, LANES), F32)
    for j in range(nblk):
        cols = slice(j * LANES, (j + 1) * LANES)
        bj = bits_s[:, cols]
        eq = bj == thr
        eq_f = jnp.where(eq, 1.0, 0.0)
        rank = off_eq + _dot(eq_f.astype(BF16), tri)
        sel = (bj > thr) | (eq & (rank < need))
        sel_f = jnp.where(sel, 1.0, 0.0)
        pos = off_sel + _dot(sel_f.astype(BF16), tri)
        pos_ref[:, cols] = jnp.where(sel, pos, -1.0)
        offs = jnp.where(lane == j, off_sel, offs)
        off_eq = off_eq + jnp.sum(eq_f, axis=1, keepdims=True)
        off_sel = off_sel + jnp.sum(sel_f, axis=1, keepdims=True)
    offs_ref[...] = jnp.where(lane == nblk, off_sel, offs).astype(I32)


def _router(h3, w_router):
    batch, seq, d = h3.shape
    cap = EC_CAPACITY * seq // N_EXPERTS
    i = jnp.arange(LANES)
    tri = jnp.where(i[:, None] < i[None, :], 1.0, 0.0).astype(BF16)
    out3 = lambda wd: pl.BlockSpec((None, N_EXPERTS, wd), lambda b: (b, 0, 0))
    return pl.pallas_call(
        functools.partial(_router_kernel, cap=cap),
        grid=(batch,),
        in_specs=[pl.BlockSpec((None, seq, d), lambda b: (b, 0, 0)),
                  pl.BlockSpec((N_EXPERTS, d), lambda b: (0, 0)),
                  pl.BlockSpec((LANES, LANES), lambda b: (0, 0))],
        out_specs=[out3(seq), out3(seq), out3(LANES)],
        out_shape=[jax.ShapeDtypeStruct((batch, N_EXPERTS, seq), F32),
                   jax.ShapeDtypeStruct((batch, N_EXPERTS, seq), F32),
                   jax.ShapeDtypeStruct((batch, N_EXPERTS, LANES), I32)],
        scratch_shapes=[pltpu.VMEM((N_EXPERTS, seq), I32)],
        compiler_params=_cparams("parallel"),
        name="router",
    )(h3, w_router.T.astype(BF16), tri)


def _moe_ffn_kernel(offs_ref, h_ref, pos_ref, aff_ref, wg_ref, wu_ref, wd_ref, ye_ref, x_s, gate_s, *, cap):
    nblk = pos_ref.shape[0]
    e = pl.program_id(0)
    b = pl.program_id(1)
    base = (b * N_EXPERTS + e) * OFFS_W
    x_s[...] = jnp.zeros_like(x_s)
    gate_s[...] = jnp.zeros_like(gate_s)
    rid = lax.broadcasted_iota(I32, (LANES, LANES), 0)
    for c in range(cap // LANES):
        rows = slice(c * LANES, (c + 1) * LANES)
        want = (rid + c * LANES).astype(F32)

        def gather(j, carry):
            lo = offs_ref[base + j]
            hi = offs_ref[base + j + 1]

            @pl.when((lo < (c + 1) * LANES) & (hi > c * LANES))
            def _():
                hit = pos_ref[pl.ds(j, 1), :] == want
                hj = h_ref[pl.ds(pl.multiple_of(j * LANES, LANES), LANES), :]
                x_s[rows, :] += _dot(jnp.where(hit, 1.0, 0.0).astype(BF16), hj)
                gate_s[rows, :] += jnp.sum(jnp.where(hit, aff_ref[pl.ds(j, 1), :], 0.0), axis=1, keepdims=True)

            return carry

        lax.fori_loop(0, nblk, gather, 0)

    half = cap // 2
    for r in range(2):
        rows = slice(r * half, (r + 1) * half)
        x = x_s[rows, :].astype(BF16)
        g = _dot(x, wg_ref[...])
        u = _dot(x, wu_ref[...])
        hid = (g * jax.nn.sigmoid(g) * u).astype(BF16)
        ye_ref[rows, :] = (_dot(hid, wd_ref[...]) * gate_s[rows, 0:1]).astype(BF16)


def _moe_ffn(offs_flat, h3, pos4, aff4, wg, wu, wd):
    batch, seq, d = h3.shape
    cap = EC_CAPACITY * seq // N_EXPERTS
    nblk = seq // LANES
    row = pl.BlockSpec((None, None, nblk, LANES), lambda e, b, o: (b, e, 0, 0))
    wspec = pl.BlockSpec((None, d, d), lambda e, b, o: (e, 0, 0))
    return pl.pallas_call(
        functools.partial(_moe_ffn_kernel, cap=cap),
        grid_spec=pltpu.PrefetchScalarGridSpec(
            num_scalar_prefetch=1,
            grid=(N_EXPERTS, batch),
            in_specs=[pl.BlockSpec((None, seq, d), lambda e, b, o: (b, 0, 0)), row, row,
                      wspec, wspec, wspec],
            out_specs=pl.BlockSpec((None, None, cap, d), lambda e, b, o: (b, e, 0, 0)),
            scratch_shapes=[pltpu.VMEM((cap, d), F32), pltpu.VMEM((cap, LANES), F32)]),
        out_shape=jax.ShapeDtypeStruct((batch, N_EXPERTS, cap, d), BF16),
        compiler_params=_cparams("arbitrary", "arbitrary"),
        name="moe_ffn",
    )(offs_flat, h3, pos4, aff4, wg, wu, wd)


def _moe_combine_kernel(offs_ref, x1_ref, ye_ref, ptok_ref, p_ref, wple_ref, wpg_ref, gple_ref, gpg_ref,
                        o_ref, acc_s, *, cap):
    tb = x1_ref.shape[0]
    b = pl.program_id(0)
    jb = pl.program_id(1)
    acc_s[...] = x1_ref[...]
    cid = lax.broadcasted_iota(I32, (LANES, LANES), 1)
    for sub in range(tb // LANES):
        rows = slice(sub * LANES, (sub + 1) * LANES)
        j = jb * (tb // LANES) + sub
        for e in range(N_EXPERTS):
            base = (b * N_EXPERTS + e) * OFFS_W
            lo = offs_ref[base + j]
            hi = offs_ref[base + j + 1]
            c0 = lax.shift_right_logical(lo, 7)
            for cc in range(2):
                c = c0 + cc

                @pl.when((hi > lo) & (c * LANES < hi))
                def _():
                    hit = ptok_ref[rows, e:e + 1] == (cid + c * LANES).astype(F32)
                    yc = ye_ref[e, pl.ds(pl.multiple_of(c * LANES, LANES), LANES), :]
                    acc_s[rows, :] += _dot(jnp.where(hit, 1.0, 0.0).astype(BF16), yc)

    x2 = acc_s[...]
    emb = _rms(_dot(p_ref[...].astype(BF16), wple_ref[...])) * gple_ref[...]
    gate = jax.nn.sigmoid(_dot((_rms(x2) * gpg_ref[...]).astype(BF16), wpg_ref[...]))
    o_ref[...] = x2 + emb * gate


def _moe_combine(offs_flat, x1_3, ye, ptok, p3, w_ple, w_pg, g_ple, g_pg, tb=256):
    batch, seq, d = x1_3.shape
    cap = ye.shape[2]
    pd = p3.shape[-1]
    const = lambda b, j, o: (0, 0)
    row = lambda wd: pl.BlockSpec((None, tb, wd), lambda b, j, o: (b, j, 0))
    return pl.pallas_call(
        functools.partial(_moe_combine_kernel, cap=cap),
        grid_spec=pltpu.PrefetchScalarGridSpec(
            num_scalar_prefetch=1,
            grid=(batch, seq // tb),
            in_specs=[row(d),
                      pl.BlockSpec((None, N_EXPERTS, cap, d), lambda b, j, o: (b, 0, 0, 0)),
                      row(N_EXPERTS), row(pd),
                      pl.BlockSpec((pd, d), const), pl.BlockSpec((d, d), const),
                      pl.BlockSpec((1, d), const), pl.BlockSpec((1, d), const)],
            out_specs=row(d),
            scratch_shapes=[pltpu.VMEM((tb, d), F32)]),
        out_shape=jax.ShapeDtypeStruct((batch, seq, d), F32),
        compiler_params=_cparams("arbitrary", "arbitrary"),
        name="moe_combine_ple",
    )(offs_flat, x1_3, ye, ptok, p3, w_ple, w_pg, g_ple.reshape(1, d), g_pg.reshape(1, d))


def _layer(x2d, p3, tables, batch, seq, prm):
    d = x2d.shape[-1]
    pa, pb, pc = _inproj(x2d, prm["g_mix"].reshape(1, d), prm["w_in"],
                         (2 * GMLP_WIDTH, 3 * DSA_WIDTH, prm["w_in"].shape[1] - 2 * GMLP_WIDTH - 3 * DSA_WIDTH))
    ya = _gmlp(pa, prm["ln_v_g"], prm["ln_v_b"], prm["w_s"], prm["b_s"])
    yb = _dsa(_dsa_prep(pb, tables, prm["q_norm_g"], prm["k_norm_g"]), batch, seq)
    yc = _gdn(pc, batch, seq, prm["conv_w"], prm["a_log"], prm["dt_bias"], prm["o_norm_g"])
    x1, h = _outproj(x2d, ya, yb, yc, prm["w_out"], prm["g_ffn"])
    h3 = h.reshape(batch, seq, d)
    pos, aff, offs = _router(h3, prm["w_router"])
    offs_flat = offs[:, :, :OFFS_W].reshape(-1)
    nblk = seq // LANES
    ye = _moe_ffn(offs_flat, h3, pos.reshape(batch, N_EXPERTS, nblk, LANES),
                  aff.reshape(batch, N_EXPERTS, nblk, LANES),
                  prm["w_e_gate"], prm["w_e_up"], prm["w_e_down"])
    x2 = _moe_combine(offs_flat, x1.reshape(batch, seq, d), ye, pos.transpose(0, 2, 1), p3,
                      prm["w_ple"], prm["w_ple_gate"], prm["g_ple"], prm["g_ple_gate"])
    return x2.reshape(batch * seq, d)


def kernel(x, p, positions, g_mix, w_in, ln_v_g, ln_v_b, w_s, b_s, q_norm_g, k_norm_g, conv_w, a_log, dt_bias,
           o_norm_g, w_out, g_ffn, w_router, w_e_gate, w_e_up, w_e_down, w_ple, g_ple, g_ple_gate, w_ple_gate):
    batch, seq, d = x.shape
    depth = p.shape[0]
    in_width = w_in.shape[-1]
    in_pad = -in_width % LANES
    w_in_b = jnp.pad(w_in, ((0, 0), (0, 0), (0, in_pad))).astype(BF16)
    tables = _rope_tables(positions)
    x2d = x.reshape(batch * seq, d)
    for i in range(depth):
        prm = dict(g_mix=g_mix[i], w_in=w_in_b[i], ln_v_g=ln_v_g[i], ln_v_b=ln_v_b[i], w_s=w_s[i], b_s=b_s[i],
                   q_norm_g=q_norm_g[i], k_norm_g=k_norm_g[i], conv_w=conv_w[i], a_log=a_log[i],
                   dt_bias=dt_bias[i], o_norm_g=o_norm_g[i], w_out=w_out[i].astype(BF16), g_ffn=g_ffn[i],
                   w_router=w_router[i], w_e_gate=w_e_gate[i].astype(BF16), w_e_up=w_e_up[i].astype(BF16),
                   w_e_down=w_e_down[i].astype(BF16), w_ple=w_ple[i].astype(BF16), g_ple=g_ple[i],
                   g_ple_gate=g_ple_gate[i], w_ple_gate=w_ple_gate[i].astype(BF16))
        x2d = _layer(x2d, p[i], tables, batch, seq, prm)
    return x2d.reshape(batch, seq, d)
```

```python
import functools

import jax
import jax.numpy as jnp
from jax import lax
from jax.experimental import pallas as pl
from jax.experimental.pallas import tpu as pltpu

F32 = jnp.float32
BF16 = jnp.bfloat16
I32 = jnp.int32

HEAD_DIM = 64
GMLP_WIDTH = 256
GMLP_GROUPS = 4
GMLP_CHUNK = 128
DSA_WIDTH = 384
DSA_PATTERNS = ((128, 1), (512, 4), (2048, 16))
GDN_WIDTH = 384
GDN_CONV = 5
ROT_DIM = 16
ROPE_THETA = 500000.0
N_EXPERTS = 16
EC_CAPACITY = 2
NORM_EPS = 1e-6
MASK_VALUE = -1e30

LANES = 128
SUBLANES = 8
VMEM_LIMIT_BYTES = 56 * 1024 * 1024

PAIR = 2 * HEAD_DIM
GDN_CHUNK = 64
GDN_GROUP = 4
ATT_HALO = 64
ATT_SUB = 128
OFFS_W = 40
MOE_TBLK = 256


def _cparams(*sem):
    return pltpu.CompilerParams(dimension_semantics=sem, vmem_limit_bytes=VMEM_LIMIT_BYTES)


def _dot(a, b):
    return jnp.dot(a, b, preferred_element_type=F32)


def _dot_nt(a, b):
    return lax.dot_general(a, b, (((1,), (1,)), ((), ())), preferred_element_type=F32)


def _dot_tn(a, b):
    return lax.dot_general(a, b, (((0,), (0,)), ((), ())), preferred_element_type=F32)


def _split2(x):
    hi = x.astype(BF16)
    lo = (x - hi.astype(F32)).astype(BF16)
    return hi, lo


def _split3(x):
    hi = x.astype(BF16)
    r = x - hi.astype(F32)
    mid = r.astype(BF16)
    lo = (r - mid.astype(F32)).astype(BF16)
    return hi, mid, lo


def _group_mean(x, avg):
    hi, lo = _split2(x)
    return _dot(hi, avg) + _dot(lo, avg)


def _rms(x):
    return x * lax.rsqrt(jnp.mean(x * x, axis=-1, keepdims=True) + NORM_EPS)


def _block_avg_matrix(width, group):
    i = jnp.arange(width)
    return jnp.where((i[:, None] // group) == (i[None, :] // group), 1.0 / group, 0.0).astype(BF16)


def _inproj_kernel(x_ref, g_ref, w_ref, *o_refs):
    x = x_ref[...]
    xn = (_rms(x) * g_ref[...]).astype(BF16)
    col = 0
    for o_ref in o_refs:
        n = o_ref.shape[-1]
        o_ref[...] = _dot(xn, w_ref[:, col:col + n])
        col += n


def _inproj(x2d, g, w, widths, tm=512):
    m, d = x2d.shape
    n = w.shape[1]
    return pl.pallas_call(
        _inproj_kernel,
        grid=(m // tm,),
        in_specs=[pl.BlockSpec((tm, d), lambda i: (i, 0)),
                  pl.BlockSpec((1, d), lambda i: (0, 0)),
                  pl.BlockSpec((d, n), lambda i: (0, 0))],
        out_specs=[pl.BlockSpec((tm, wd), lambda i: (i, 0)) for wd in widths],
        out_shape=[jax.ShapeDtypeStruct((m, wd), F32) for wd in widths],
        compiler_params=_cparams("parallel"),
        name="inproj",
    )(x2d, g, w)


def _gmlp_kernel(u_ref, v_ref, lng_ref, lnb_ref, ws_ref, bias_ref, avg_ref, o_ref):
    tm = u_ref.shape[0]
    u = jax.nn.gelu(u_ref[...])
    vf = jax.nn.gelu(v_ref[...])
    avg = avg_ref[...]
    mu = _group_mean(vf, avg)
    dv = vf - mu
    var = _group_mean(dv * dv, avg)
    vn = dv * lax.rsqrt(var + NORM_EPS) * lng_ref[...] + lnb_ref[...]
    grp = lax.broadcasted_iota(I32, (GMLP_CHUNK, GMLP_WIDTH), 1) // HEAD_DIM
    for c in range(tm // GMLP_CHUNK):
        rows = slice(c * GMLP_CHUNK, (c + 1) * GMLP_CHUNK)
        vc = vn[rows].astype(BF16)
        mixed = bias_ref[...]
        for g in range(GMLP_GROUPS):
            mixed = mixed + jnp.where(grp == g, _dot(ws_ref[g], vc), 0.0)
        o_ref[rows, :] = (u[rows] * mixed).astype(BF16)


def _gmlp(pa, ln_g, ln_b, w_s, b_s, tm=512):
    m = pa.shape[0]
    w = GMLP_WIDTH
    bias2d = jnp.repeat(b_s.T, HEAD_DIM, axis=1)
    const = lambda i: (0, 0)
    return pl.pallas_call(
        _gmlp_kernel,
        grid=(m // tm,),
        in_specs=[pl.BlockSpec((tm, w), lambda i: (i, 0)),
                  pl.BlockSpec((tm, w), lambda i: (i, 1)),
                  pl.BlockSpec((1, w), const),
                  pl.BlockSpec((1, w), const),
                  pl.BlockSpec((GMLP_GROUPS, GMLP_CHUNK, GMLP_CHUNK), lambda i: (0, 0, 0)),
                  pl.BlockSpec((GMLP_CHUNK, w), const),
                  pl.BlockSpec((w, w), const)],
        out_specs=pl.BlockSpec((tm, w), lambda i: (i, 0)),
        out_shape=jax.ShapeDtypeStruct((m, w), BF16),
        compiler_params=_cparams("parallel"),
        name="gmlp",
    )(pa, pa, ln_g.reshape(1, w), ln_b.reshape(1, w), w_s.astype(BF16), bias2d,
      _block_avg_matrix(w, HEAD_DIM))


def _rope_table_kernel(pos_ref, invf_ref, sa_ref, sb_ref, cos_ref, sina_ref, sinb_ref):
    ang = pos_ref[...].astype(F32) * invf_ref[...]
    s = jnp.sin(ang)
    cos_ref[...] = jnp.cos(ang)
    sina_ref[...] = s * sa_ref[...]
    sinb_ref[...] = s * sb_ref[...]


def _rope_tables(positions, ts=1024):
    m = positions.size
    half = ROT_DIM // 2
    inv_freq = ROPE_THETA ** (-jnp.arange(half, dtype=F32) * 2.0 / ROT_DIM)
    lane = jnp.arange(PAIR) % HEAD_DIM
    invf = jnp.where(lane < ROT_DIM, inv_freq[lane % half], 0.0).reshape(1, PAIR)
    sa = jnp.where(lane < half, -1.0, 0.0).astype(F32).reshape(1, PAIR)
    sb = jnp.where((lane >= half) & (lane < ROT_DIM), 1.0, 0.0).astype(F32).reshape(1, PAIR)
    const = lambda i: (0, 0)
    row = pl.BlockSpec((ts, PAIR), lambda i: (i, 0))
    return pl.pallas_call(
        _rope_table_kernel,
        grid=(m // ts,),
        in_specs=[pl.BlockSpec((ts, 1), lambda i: (i, 0)),
                  pl.BlockSpec((1, PAIR), const), pl.BlockSpec((1, PAIR), const),
                  pl.BlockSpec((1, PAIR), const)],
        out_specs=[row, row, row],
        out_shape=[jax.ShapeDtypeStruct((m, PAIR), F32)] * 3,
        compiler_params=_cparams("parallel"),
        name="rope_tables",
    )(positions.reshape(m, 1), invf, sa, sb)


def _dsa_prep_kernel(q_ref, k_ref, v_ref, cos_ref, sina_ref, sinb_ref, gq_ref, gk_ref, avg_ref, o_ref):
    w = DSA_WIDTH
    cos, sina, sinb = cos_ref[...], sina_ref[...], sinb_ref[...]
    avg = avg_ref[...]
    half = ROT_DIM // 2

    def norm_rot(t, g, scale, col0):
        t = t * lax.rsqrt(_group_mean(t * t, avg) + NORM_EPS) * g
        for hp in range(w // PAIR):
            tp = t[:, hp * PAIR:(hp + 1) * PAIR]
            rot = tp * cos + pltpu.roll(tp, PAIR - half, 1) * sina + pltpu.roll(tp, half, 1) * sinb
            o_ref[:, col0 + hp * PAIR:col0 + (hp + 1) * PAIR] = (rot * scale).astype(BF16)

    norm_rot(q_ref[...], gq_ref[...], HEAD_DIM ** -0.5, 0)
    norm_rot(k_ref[...], gk_ref[...], 1.0, w)
    o_ref[:, 2 * w:3 * w] = v_ref[...].astype(BF16)


def _dsa_prep(pb, tables, q_norm_g, k_norm_g, tm=512):
    m = pb.shape[0]
    w = DSA_WIDTH
    const = lambda i: (0, 0)
    tab = pl.BlockSpec((tm, PAIR), lambda i: (i, 0))
    gq = jnp.tile(q_norm_g, w // HEAD_DIM).reshape(1, w)
    gk = jnp.tile(k_norm_g, w // HEAD_DIM).reshape(1, w)
    return pl.pallas_call(
        _dsa_prep_kernel,
        grid=(m // tm,),
        in_specs=[pl.BlockSpec((tm, w), lambda i: (i, 0)),
                  pl.BlockSpec((tm, w), lambda i: (i, 1)),
                  pl.BlockSpec((tm, w), lambda i: (i, 2)),
                  tab, tab, tab,
                  pl.BlockSpec((1, w), const), pl.BlockSpec((1, w), const),
                  pl.BlockSpec((w, w), const)],
        out_specs=pl.BlockSpec((tm, 3 * w), lambda i: (i, 0)),
        out_shape=jax.ShapeDtypeStruct((m, 3 * w), BF16),
        compiler_params=_cparams("parallel"),
        name="dsa_prep",
    )(pb, pb, pb, *tables, gq, gk, _block_avg_matrix(w, HEAD_DIM))


def _dsa_kernel(*refs, steps, sub_len, first, last):
    q_ref, kp_ref, k_ref, kn_ref, vp_ref, v_ref, vn_ref = refs[:7]
    refs = refs[7:]
    if not first:
        acc_in_ref, ml_in_ref = refs[:2]
        refs = refs[2:]
    if last:
        o_ref, kbuf, vbuf = refs
    else:
        acc_out_ref, ml_out_ref, kbuf, vbuf = refs
    t = q_ref.shape[0]
    i = pl.program_id(2)
    kbuf[0:ATT_HALO, :] = kp_ref[...]
    kbuf[ATT_HALO:ATT_HALO + t, :] = k_ref[...]
    kbuf[ATT_HALO + t:, :] = kn_ref[...]
    vbuf[0:ATT_HALO, :] = vp_ref[...]
    vbuf[ATT_HALO:ATT_HALO + t, :] = v_ref[...]
    vbuf[ATT_HALO + t:, :] = vn_ref[...]

    nk = ATT_SUB + 2 * ATT_HALO
    lane = lax.broadcasted_iota(I32, (1, PAIR), 1)
    head_mask = (lane < HEAD_DIM, lane >= HEAD_DIM)
    lane_ml = lax.broadcasted_iota(I32, (ATT_SUB, LANES), 1)
    for s in range(t // ATT_SUB):
        rows = slice(s * ATT_SUB, (s + 1) * ATT_SUB)
        base = i * t + s * ATT_SUB
        qi = base + lax.broadcasted_iota(I32, (ATT_SUB, 1), 0)
        kj = base - ATT_HALO + lax.broadcasted_iota(I32, (1, nk), 1)
        valid = (jnp.abs(kj - qi) <= steps) & (kj >= 0) & (kj < sub_len)
        kk = kbuf[s * ATT_SUB:s * ATT_SUB + nk, :]
        vv = vbuf[s * ATT_SUB:s * ATT_SUB + nk, :]
        ml_new = jnp.zeros((ATT_SUB, LANES), F32)
        for hp in range(DSA_WIDTH // PAIR):
            cols = slice(hp * PAIR, (hp + 1) * PAIR)
            qp = q_ref[rows, cols]
            kp = kk[:, cols]
            vp = vv[:, cols]
            acc = jnp.zeros((ATT_SUB, PAIR), F32)
            scale = []
            for hh in range(2):
                h = 2 * hp + hh
                km = jnp.where(head_mask[hh], kp, jnp.zeros_like(kp))
                sc = jnp.where(valid, _dot_nt(qp, km), MASK_VALUE)
                m_new = jnp.max(sc, axis=1, keepdims=True)
                if not first:
                    m_prev = ml_in_ref[rows, h:h + 1]
                    l_prev = ml_in_ref[rows, 8 + h:9 + h]
                    m_new = jnp.maximum(m_prev, m_new)
                    alpha = jnp.exp(m_prev - m_new)
                p = jnp.exp(sc - m_new)
                l_new = jnp.sum(p, axis=1, keepdims=True)
                if not first:
                    l_new = l_new + alpha * l_prev
                    scale.append(alpha)
                vm = jnp.where(head_mask[hh], vp, jnp.zeros_like(vp))
                acc = acc + _dot(p.astype(BF16), vm)
                if last:
                    scale.append(1.0 / l_new)
                else:
                    ml_new = jnp.where(lane_ml == h, m_new, jnp.where(lane_ml == 8 + h, l_new, ml_new))
            if not first:
                prev = acc_in_ref[rows, cols]
                if last:
                    acc = acc + prev * jnp.where(head_mask[0], scale[0], scale[2])
                    acc = acc * jnp.where(head_mask[0], scale[1], scale[3])
                else:
                    acc = acc + prev * jnp.where(head_mask[0], scale[0], scale[1])
            elif last:
                acc = acc * jnp.where(head_mask[0], scale[0], scale[1])
            if last:
                o_ref[rows, cols] = acc.astype(BF16)
            else:
                acc_out_ref[rows, cols] = acc
        if not last:
            ml_out_ref[rows, :] = ml_new


def _dsa_pattern(qkv, state, batch, seq, window, dil, first, last):
    w = DSA_WIDTH
    steps = window // (2 * dil)
    assert steps <= ATT_HALO
    sub_len = seq // dil
    t = min(512, sub_len)
    nb64 = sub_len // ATT_HALO
    r64 = t // ATT_HALO
    grid = (batch, dil, sub_len // t)

    def view(a, width):
        return a.reshape(batch, sub_len, dil * width)

    def main(which):
        return pl.BlockSpec((None, t, w), lambda b, r, i: (b, i, r * 3 + which))

    def prev(which):
        return pl.BlockSpec((None, ATT_HALO, w),
                            lambda b, r, i: (b, jnp.maximum(i * r64 - 1, 0), r * 3 + which))

    def nxt(which):
        return pl.BlockSpec((None, ATT_HALO, w),
                            lambda b, r, i: (b, jnp.minimum((i + 1) * r64, nb64 - 1), r * 3 + which))

    acc_spec = pl.BlockSpec((None, t, w), lambda b, r, i: (b, i, r))
    ml_spec = pl.BlockSpec((None, t, LANES), lambda b, r, i: (b, i, r))
    qv = view(qkv, 3 * w)
    in_specs = [main(0), prev(1), main(1), nxt(1), prev(2), main(2), nxt(2)]
    args = [qv] * 7
    if not first:
        in_specs += [acc_spec, ml_spec]
        args += [view(state[0], w), view(state[1], LANES)]
    if last:
        out_specs = acc_spec
        out_shape = jax.ShapeDtypeStruct((batch, sub_len, dil * w), BF16)
    else:
        out_specs = [acc_spec, ml_spec]
        out_shape = [jax.ShapeDtypeStruct((batch, sub_len, dil * w), F32),
                     jax.ShapeDtypeStruct((batch, sub_len, dil * LANES), F32)]
    out = pl.pallas_call(
        functools.partial(_dsa_kernel, steps=steps, sub_len=sub_len, first=first, last=last),
        grid=grid,
        in_specs=in_specs,
        out_specs=out_specs,
        out_shape=out_shape,
        scratch_shapes=[pltpu.VMEM((t + 2 * ATT_HALO, w), BF16),
                        pltpu.VMEM((t + 2 * ATT_HALO, w), BF16)],
        compiler_params=_cparams("parallel", "parallel", "parallel"),
        name=f"dsa_d{dil}",
    )(*args)
    if last:
        return out.reshape(batch * seq, w)
    return out[0].reshape(batch * seq, w), out[1].reshape(batch * seq, LANES)


def _dsa(qkv, batch, seq):
    state = None
    n = len(DSA_PATTERNS)
    for idx, (window, dil) in enumerate(DSA_PATTERNS):
        state = _dsa_pattern(qkv, state, batch, seq, window, dil, idx == 0, idx == n - 1)
    return state


def _gdn_kernel(xq_ref, xk_ref, xv_ref, gate_ref, ab_ref, cw_ref, alog_ref, dtb_ref, on_ref,
                ones_ref, avg_ref, o_ref, m_s, n_s, qp_s, op_s, el_s):
    seq = xq_ref.shape[0]
    c = GDN_CHUNK
    nch = seq // c
    halo = SUBLANES
    lane = lax.broadcasted_iota(I32, (1, PAIR), 1)
    head_mask = (lane < HEAD_DIM, lane >= HEAD_DIM)
    ti = lax.broadcasted_iota(I32, (c, c), 0)
    si = lax.broadcasted_iota(I32, (c, c), 1)
    incl = (si <= ti, si >= ti)
    strict = (si < ti, si > ti)
    tri_incl = tuple(jnp.where(m, 1.0, 0.0).astype(BF16) for m in incl)
    bd_ones = ones_ref[...]
    lane8 = lax.broadcasted_iota(I32, (1, LANES), 1)
    neg_a = -jnp.exp(alog_ref[...])
    dtb = dtb_ref[...]

    def conv_silu(x_ref, which, r0, ci):
        main = x_ref[pl.ds(r0, c), :]
        before = x_ref[pl.ds(jnp.maximum(r0 - halo, 0), halo), :]
        after = x_ref[pl.ds(jnp.minimum(r0 + c, seq - halo), halo), :]
        before = jnp.where(ci > 0, before, 0.0)
        after = jnp.where(ci < nch - 1, after, 0.0)
        win = jnp.concatenate([before, main, after], axis=0)
        n = c + 2 * halo
        y = jnp.zeros((c, PAIR), F32)
        for j in range(GDN_CONV):
            shifted = win if j == GDN_CONV // 2 else pltpu.roll(win, (GDN_CONV // 2 - j) % n, 0)
            y = y + shifted[halo:halo + c] * cw_ref[which, j:j + 1, :]
        return y * jax.nn.sigmoid(y)

    def l2n(x):
        hi, lo = _split2(x * x)
        ss = _dot(hi, bd_ones) + _dot(lo, bd_ones)
        return x * lax.rsqrt(ss + NORM_EPS)

    eye = jnp.where(si == ti, 1.0, 0.0)
    li = lax.broadcasted_iota(I32, (PAIR, PAIR), 0) // HEAD_DIM
    lj = lax.broadcasted_iota(I32, (PAIR, PAIR), 1) // HEAD_DIM
    same_head = li == lj

    def transform(it, carry):
        dirs = []
        for gi in range(GDN_GROUP):
            ci = it * GDN_GROUP + gi
            r0 = pl.multiple_of(ci * c, c)
            q = l2n(conv_silu(xq_ref, 0, r0, ci)) * (HEAD_DIM ** -0.5)
            k = l2n(conv_silu(xk_ref, 1, r0, ci))
            v = conv_silu(xv_ref, 2, r0, ci)
            ab = ab_ref[pl.ds(r0, c), :]
            gb = jnp.where(lane8 < 4, neg_a * jax.nn.softplus(ab + dtb), jax.nn.sigmoid(ab))
            g3 = _split3(gb)
            for d in range(2):
                dirs.append(dict(ci=ci, r0=r0, d=d, q=q, k=k, v=v, gb=gb, g3=g3))
        for dd in dirs:
            dd["gc"] = sum(_dot(tri_incl[dd["d"]], part) for part in dd["g3"])
        chains = []
        for dd in dirs:
            d, gc, gb, k = dd["d"], dd["gc"], dd["gb"], dd["k"]
            gct = gc.T
            g_last = gc[c - 1:c, :] if d == 0 else gc[0:1, :]
            dd["chains"] = []
            for hh in range(2):
                col = 2 * d + hh
                gc_c = gc[:, col:col + 1]
                beta = gb[:, 4 + col:5 + col]
                km = jnp.where(head_mask[hh], k, 0.0)
                kb = km * beta
                eg = jnp.exp(gc_c)
                gl = g_last[:, col:col + 1]
                ch = dict(d=d,
                          decay=jnp.where(incl[d], jnp.exp(jnp.minimum(gc_c - gct[col:col + 1, :], 0.0)), 0.0),
                          kmb=km.astype(BF16), kbb=kb.astype(BF16),
                          qmb=jnp.where(head_mask[hh], dd["q"], 0.0).astype(BF16),
                          x0=jnp.concatenate([jnp.where(head_mask[hh], dd["v"], 0.0) * beta, kb * eg],
                                             axis=1).astype(BF16),
                          eg=eg, erem=jnp.exp(gl - gc_c), elast=jnp.exp(gl))
                chains.append(ch)
                dd["chains"].append(ch)
        for ch in chains:
            ch["a"] = _dot_nt(ch["kbb"], ch["kmb"])
        for ch in chains:
            ch["qkr"] = _dot_nt(ch["qmb"], ch["kmb"])
        for ch in chains:
            lmat = jnp.where(strict[ch["d"]], ch["a"] * ch["decay"], 0.0)
            ch["pw"] = lmat.astype(BF16)
            ch["t"] = eye - lmat
            ch["qk"] = jnp.where(incl[ch["d"]], ch["qkr"] * ch["decay"], 0.0).astype(BF16)
        for ch in chains:
            ch["sq"] = _dot(ch["pw"], ch["pw"])
        for ch in chains:
            ch["pw"] = ch["sq"].astype(BF16)
        for _ in range(4):
            for ch in chains:
                ch["tp"] = _dot(ch["t"].astype(BF16), ch["pw"])
            for ch in chains:
                ch["sq"] = _dot(ch["pw"], ch["pw"])
            for ch in chains:
                ch["t"] = ch["t"] + ch["tp"]
                ch["pw"] = ch["sq"].astype(BF16)
        for ch in chains:
            ch["tp"] = _dot(ch["t"].astype(BF16), ch["pw"])
        for ch in chains:
            ch["xu"] = _dot((ch["t"] + ch["tp"]).astype(BF16), ch["x0"])
        for dd in dirs:
            d, r0, ci = dd["d"], dd["r0"], dd["ci"]
            c0, c1 = dd["chains"]
            u_h = [c0["xu"][:, :PAIR].astype(BF16), c1["xu"][:, :PAIR].astype(BF16)]
            w_h = [c0["xu"][:, PAIR:].astype(BF16), c1["xu"][:, PAIR:].astype(BF16)]
            kd = (dd["k"] * jnp.where(head_mask[0], c0["erem"], c1["erem"])).astype(BF16)
            m_s[d, ci] = jnp.where(same_head, _dot_tn(kd, w_h[0] + w_h[1]), 0.0).astype(BF16)
            n_s[d, ci] = jnp.where(same_head, _dot_tn(kd, u_h[0] + u_h[1]), 0.0)
            qg = dd["q"] * jnp.where(head_mask[0], c0["eg"], c1["eg"])
            qp_s[d, pl.ds(r0, c), :] = (qg - _dot(c0["qk"], w_h[0]) - _dot(c1["qk"], w_h[1])).astype(BF16)
            op_s[d, pl.ds(r0, c), :] = _dot(c0["qk"], u_h[0]) + _dot(c1["qk"], u_h[1])
            el_s[d, pl.ds(ci, 1), :] = jnp.where(head_mask[0], c0["elast"], c1["elast"])
        return carry

    lax.fori_loop(0, nch // GDN_GROUP, transform, 0)

    def scan(i, states):
        cis = (i, nch - 1 - i)
        rows = [pl.ds(pl.multiple_of(ci * c, c), c) for ci in cis]
        stb = [st.astype(BF16) for st in states]
        trans = [_dot(m_s[d, cis[d]], stb[d]) for d in range(2)]
        outs = [_dot(qp_s[d, rows[d], :], stb[d]) for d in range(2)]
        new_states = []
        for d in range(2):
            op_s[d, rows[d], :] += outs[d]
            new_states.append(states[d] * el_s[d, pl.ds(cis[d], 1), :] - trans[d] + n_s[d, cis[d]])
        return tuple(new_states)

    zero = jnp.zeros((PAIR, PAIR), F32)
    lax.fori_loop(0, nch, scan, (zero, zero))

    avg = avg_ref[...]
    tile = 512

    def finish(ti_, carry):
        r0 = pl.multiple_of(ti_ * tile, tile)
        o = op_s[0, pl.ds(r0, tile), :] + op_s[1, pl.ds(r0, tile), :]
        o = o * lax.rsqrt(_group_mean(o * o, avg) + NORM_EPS) * on_ref[...]
        gate = gate_ref[pl.ds(r0, tile), :]
        o_ref[pl.ds(r0, tile), :] = (o * (gate * jax.nn.sigmoid(gate))).astype(BF16)
        return carry

    lax.fori_loop(0, seq // tile, finish, 0)


def _gdn(pc, batch, seq, conv_w, a_log, dt_bias, o_norm_g):
    w = GDN_WIDTH
    heads = w // HEAD_DIM
    pairs = w // PAIR
    pc3 = pc.reshape(batch, seq, pc.shape[-1])
    ab = pc3[:, :, 4 * w:4 * w + 4 * heads].reshape(batch, seq, 2, 2, pairs, 2)
    ab = ab.transpose(0, 4, 1, 2, 3, 5).reshape(batch, pairs, seq, 8)
    ab = jnp.pad(ab, ((0, 0), (0, 0), (0, 0), (0, LANES - 8)))
    cw = conv_w.reshape(GDN_CONV, 3, w).transpose(1, 0, 2)
    cw = jnp.pad(cw, ((0, 0), (0, SUBLANES - GDN_CONV), (0, 0)))

    def pair_lanes(p):
        x = p.reshape(2, pairs, 2).transpose(1, 0, 2).reshape(pairs, 1, 4)
        return jnp.pad(x, ((0, 0), (0, 0), (0, LANES - 4)))

    nblk = w // PAIR

    def col(offset):
        return pl.BlockSpec((None, seq, PAIR), lambda b, hp: (b, 0, offset + hp))

    nch = seq // GDN_CHUNK
    return pl.pallas_call(
        _gdn_kernel,
        grid=(batch, pairs),
        in_specs=[col(0), col(nblk), col(2 * nblk), col(3 * nblk),
                  pl.BlockSpec((None, None, seq, LANES), lambda b, hp: (b, hp, 0, 0)),
                  pl.BlockSpec((3, SUBLANES, PAIR), lambda b, hp: (0, 0, hp)),
                  pl.BlockSpec((None, 1, LANES), lambda b, hp: (hp, 0, 0)),
                  pl.BlockSpec((None, 1, LANES), lambda b, hp: (hp, 0, 0)),
                  pl.BlockSpec((1, PAIR), lambda b, hp: (0, 0)),
                  pl.BlockSpec((PAIR, PAIR), lambda b, hp: (0, 0)),
                  pl.BlockSpec((PAIR, PAIR), lambda b, hp: (0, 0))],
        out_specs=pl.BlockSpec((None, seq, PAIR), lambda b, hp: (b, 0, hp)),
        out_shape=jax.ShapeDtypeStruct((batch, seq, w), BF16),
        scratch_shapes=[pltpu.VMEM((2, nch, PAIR, PAIR), BF16),
                        pltpu.VMEM((2, nch, PAIR, PAIR), F32),
                        pltpu.VMEM((2, seq, PAIR), BF16),
                        pltpu.VMEM((2, seq, PAIR), F32),
                        pltpu.VMEM((2, nch, PAIR), F32)],
        compiler_params=_cparams("parallel", "parallel"),
        name="gdn",
    )(pc3, pc3, pc3, pc3, ab, cw, pair_lanes(a_log), pair_lanes(dt_bias),
      jnp.tile(o_norm_g, 2).reshape(1, PAIR),
      (_block_avg_matrix(PAIR, HEAD_DIM) * HEAD_DIM).astype(BF16),
      _block_avg_matrix(PAIR, HEAD_DIM)).reshape(batch * seq, w)


def _outproj_kernel(x_ref, ya_ref, yb_ref, yc_ref, wa_ref, wb_ref, wc_ref, g_ref, x1_ref, h_ref):
    x1 = (x_ref[...] + _dot(ya_ref[...], wa_ref[...]) + _dot(yb_ref[...], wb_ref[...])
          + _dot(yc_ref[...], wc_ref[...]))
    x1_ref[...] = x1
    h_ref[...] = (_rms(x1) * g_ref[...]).astype(BF16)


def _outproj(x2d, ya, yb, yc, w_out, g_ffn, tm=512):
    m, d = x2d.shape
    wa = w_out[:GMLP_WIDTH]
    wb = w_out[GMLP_WIDTH:GMLP_WIDTH + DSA_WIDTH]
    wc = w_out[GMLP_WIDTH + DSA_WIDTH:]
    const = lambda i: (0, 0)
    row = lambda wd: pl.BlockSpec((tm, wd), lambda i: (i, 0))
    return pl.pallas_call(
        _outproj_kernel,
        grid=(m // tm,),
        in_specs=[row(d), row(GMLP_WIDTH), row(DSA_WIDTH), row(GDN_WIDTH),
                  pl.BlockSpec(wa.shape, const), pl.BlockSpec(wb.shape, const),
                  pl.BlockSpec(wc.shape, const), pl.BlockSpec((1, d), const)],
        out_specs=[row(d), row(d)],
        out_shape=[jax.ShapeDtypeStruct((m, d), F32), jax.ShapeDtypeStruct((m, d), BF16)],
        compiler_params=_cparams("parallel"),
        name="outproj",
    )(x2d, ya, yb, yc, wa, wb, wc, g_ffn.reshape(1, d))


def _router_kernel(h_ref, wr_ref, tri_ref, pos_ref, aff_ref, offs_ref, bits_s, *, cap):
    seq = h_ref.shape[0]
    nblk = seq // LANES
    logits = _dot_nt(wr_ref[...], h_ref[...])
    ex = jnp.exp(logits - jnp.max(logits, axis=0, keepdims=True))
    aff = ex / jnp.sum(ex, axis=0, keepdims=True)
    aff_ref[...] = aff
    bits = lax.bitcast_convert_type(aff, I32)
    bits_s[...] = bits

    def bisect(it, prefix):
        cand = prefix | jnp.left_shift(jnp.int32(1), 30 - it)
        cnt = jnp.sum(jnp.where(bits_s[...] >= cand, 1.0, 0.0), axis=1, keepdims=True)
        return jnp.where(cnt >= cap, cand, prefix)

    thr = lax.fori_loop(0, 31, bisect, jnp.zeros((N_EXPERTS, 1), I32))
    n_gt = jnp.sum(jnp.where(bits > thr, 1.0, 0.0), axis=1, keepdims=True)
    need = cap - n_gt
    tri = tri_ref[...]
    lane = lax.broadcasted_iota(I32, (N_EXPERTS, LANES), 1)

    off_eq = jnp.zeros((N_EXPERTS, 1), F32)
    off_sel = jnp.zeros((N_EXPERTS, 1), F32)
    offs = jnp.zeros((N_EXPERTS, LANES), F32)
    for j in range(nblk):
        cols = slice(j * LANES, (j + 1) * LANES)
        bj = bits_s[:, cols]
        eq = bj == thr
        eq_f = jnp.where(eq, 1.0, 0.0)
        rank = off_eq + _dot(eq_f.astype(BF16), tri)
        sel = (bj > thr) | (eq & (rank < need))
        sel_f = jnp.where(sel, 1.0, 0.0)
        pos = off_sel + _dot(sel_f.astype(BF16), tri)
        pos_ref[:, cols] = jnp.where(sel, pos, -1.0)
        offs = jnp.where(lane == j, off_sel, offs)
        off_eq = off_eq + jnp.sum(eq_f, axis=1, keepdims=True)
        off_sel = off_sel + jnp.sum(sel_f, axis=1, keepdims=True)
    offs_ref[...] = jnp.where(lane == nblk, off_sel, offs).astype(I32)


def _router(h3, w_router):
    batch, seq, d = h3.shape
    cap = EC_CAPACITY * seq // N_EXPERTS
    i = jnp.arange(LANES)
    tri = jnp.where(i[:, None] < i[None, :], 1.0, 0.0).astype(BF16)
    out3 = lambda wd: pl.BlockSpec((None, N_EXPERTS, wd), lambda b: (b, 0, 0))
    return pl.pallas_call(
        functools.partial(_router_kernel, cap=cap),
        grid=(batch,),
        in_specs=[pl.BlockSpec((None, seq, d), lambda b: (b, 0, 0)),
                  pl.BlockSpec((N_EXPERTS, d), lambda b: (0, 0)),
                  pl.BlockSpec((LANES, LANES), lambda b: (0, 0))],
        out_specs=[out3(seq), out3(seq), out3(LANES)],
        out_shape=[jax.ShapeDtypeStruct((batch, N_EXPERTS, seq), F32),
                   jax.ShapeDtypeStruct((batch, N_EXPERTS, seq), F32),
                   jax.ShapeDtypeStruct((batch, N_EXPERTS, LANES), I32)],
        scratch_shapes=[pltpu.VMEM((N_EXPERTS, seq), I32)],
        compiler_params=_cparams("parallel"),
        name="router",
    )(h3, w_router.T.astype(BF16), tri)


def _moe_ffn_kernel(offs_ref, h_ref, pos_ref, aff_ref, wg_ref, wu_ref, wd_ref, ye_ref, x_s, gate_s, *, cap):
    nblk, tblk = pos_ref.shape
    per = tblk // LANES
    e = pl.program_id(0)
    b = pl.program_id(1)
    base = (b * N_EXPERTS + e) * OFFS_W
    x_s[...] = jnp.zeros_like(x_s)
    gate_s[...] = jnp.zeros_like(gate_s)
    rid = lax.broadcasted_iota(I32, (LANES, tblk), 0)
    for c in range(cap // LANES):
        rows = slice(c * LANES, (c + 1) * LANES)
        want = (rid + c * LANES).astype(F32)

        def gather(j, carry):
            lo = offs_ref[base + j * per]
            hi = offs_ref[base + (j + 1) * per]

            @pl.when((lo < (c + 1) * LANES) & (hi > c * LANES))
            def _():
                hit = pos_ref[pl.ds(j, 1), :] == want
                hj = h_ref[pl.ds(pl.multiple_of(j * tblk, tblk), tblk), :]
                x_s[rows, :] += _dot(jnp.where(hit, 1.0, 0.0).astype(BF16), hj)
                gate_s[rows, :] += jnp.sum(jnp.where(hit, aff_ref[pl.ds(j, 1), :], 0.0), axis=1, keepdims=True)

            return carry

        lax.fori_loop(0, nblk, gather, 0)

    half = cap // 2
    for r in range(2):
        rows = slice(r * half, (r + 1) * half)
        x = x_s[rows, :].astype(BF16)
        g = _dot(x, wg_ref[...])
        u = _dot(x, wu_ref[...])
        hid = (g * jax.nn.sigmoid(g) * u).astype(BF16)
        ye_ref[rows, :] = (_dot(hid, wd_ref[...]) * gate_s[rows, 0:1]).astype(BF16)


def _moe_ffn(offs_flat, h3, pos4, aff4, wg, wu, wd):
    batch, seq, d = h3.shape
    cap = EC_CAPACITY * seq // N_EXPERTS
    nblk, tblk = pos4.shape[2:]
    row = pl.BlockSpec((None, None, nblk, tblk), lambda e, b, o: (b, e, 0, 0))
    wspec = pl.BlockSpec((None, d, d), lambda e, b, o: (e, 0, 0))
    return pl.pallas_call(
        functools.partial(_moe_ffn_kernel, cap=cap),
        grid_spec=pltpu.PrefetchScalarGridSpec(
            num_scalar_prefetch=1,
            grid=(N_EXPERTS, batch),
            in_specs=[pl.BlockSpec((None, seq, d), lambda e, b, o: (b, 0, 0)), row, row,
                      wspec, wspec, wspec],
            out_specs=pl.BlockSpec((None, None, cap, d), lambda e, b, o: (b, e, 0, 0)),
            scratch_shapes=[pltpu.VMEM((cap, d), F32), pltpu.VMEM((cap, LANES), F32)]),
        out_shape=jax.ShapeDtypeStruct((batch, N_EXPERTS, cap, d), BF16),
        compiler_params=_cparams("arbitrary", "arbitrary"),
        name="moe_ffn",
    )(offs_flat, h3, pos4, aff4, wg, wu, wd)


def _moe_combine_kernel(offs_ref, x1_ref, ye_ref, ptok_ref, p_ref, wple_ref, wpg_ref, gple_ref, gpg_ref,
                        o_ref, acc_s, *, cap):
    tb = x1_ref.shape[0]
    b = pl.program_id(0)
    jb = pl.program_id(1)
    win = 2 * LANES
    cid = lax.broadcasted_iota(I32, (LANES, win), 1)
    for sub in range(tb // LANES):
        rows = slice(sub * LANES, (sub + 1) * LANES)
        j = jb * (tb // LANES) + sub
        acc = x1_ref[rows, :]
        for e in range(N_EXPERTS):
            lo = offs_ref[(b * N_EXPERTS + e) * OFFS_W + j]
            start = jnp.minimum(lax.shift_right_logical(lo, 7) * LANES, cap - win)
            start = pl.multiple_of(start, LANES)
            hit = ptok_ref[rows, e:e + 1] == (cid + start).astype(F32)
            acc = acc + _dot(jnp.where(hit, 1.0, 0.0).astype(BF16), ye_ref[e, pl.ds(start, win), :])
        acc_s[rows, :] = acc

    x2 = acc_s[...]
    emb = _rms(_dot(p_ref[...].astype(BF16), wple_ref[...])) * gple_ref[...]
    gate = jax.nn.sigmoid(_dot((_rms(x2) * gpg_ref[...]).astype(BF16), wpg_ref[...]))
    o_ref[...] = x2 + emb * gate


def _moe_combine(offs_flat, x1_3, ye, ptok, p3, w_ple, w_pg, g_ple, g_pg, tb=256):
    batch, seq, d = x1_3.shape
    cap = ye.shape[2]
    pd = p3.shape[-1]
    const = lambda b, j, o: (0, 0)
    row = lambda wd: pl.BlockSpec((None, tb, wd), lambda b, j, o: (b, j, 0))
    return pl.pallas_call(
        functools.partial(_moe_combine_kernel, cap=cap),
        grid_spec=pltpu.PrefetchScalarGridSpec(
            num_scalar_prefetch=1,
            grid=(batch, seq // tb),
            in_specs=[row(d),
                      pl.BlockSpec((None, N_EXPERTS, cap, d), lambda b, j, o: (b, 0, 0, 0)),
                      row(N_EXPERTS), row(pd),
                      pl.BlockSpec((pd, d), const), pl.BlockSpec((d, d), const),
                      pl.BlockSpec((1, d), const), pl.BlockSpec((1, d), const)],
            out_specs=row(d),
            scratch_shapes=[pltpu.VMEM((tb, d), F32)]),
        out_shape=jax.ShapeDtypeStruct((batch, seq, d), F32),
        compiler_params=_cparams("arbitrary", "arbitrary"),
        name="moe_combine_ple",
    )(offs_flat, x1_3, ye, ptok, p3, w_ple, w_pg, g_ple.reshape(1, d), g_pg.reshape(1, d))


def _layer(x2d, p3, tables, batch, seq, prm):
    d = x2d.shape[-1]
    pa, pb, pc = _inproj(x2d, prm["g_mix"].reshape(1, d), prm["w_in"],
                         (2 * GMLP_WIDTH, 3 * DSA_WIDTH, prm["w_in"].shape[1] - 2 * GMLP_WIDTH - 3 * DSA_WIDTH))
    ya = _gmlp(pa, prm["ln_v_g"], prm["ln_v_b"], prm["w_s"], prm["b_s"])
    yb = _dsa(_dsa_prep(pb, tables, prm["q_norm_g"], prm["k_norm_g"]), batch, seq)
    yc = _gdn(pc, batch, seq, prm["conv_w"], prm["a_log"], prm["dt_bias"], prm["o_norm_g"])
    x1, h = _outproj(x2d, ya, yb, yc, prm["w_out"], prm["g_ffn"])
    h3 = h.reshape(batch, seq, d)
    pos, aff, offs = _router(h3, prm["w_router"])
    offs_flat = offs[:, :, :OFFS_W].reshape(-1)
    nblk = seq // MOE_TBLK
    ye = _moe_ffn(offs_flat, h3, pos.reshape(batch, N_EXPERTS, nblk, MOE_TBLK),
                  aff.reshape(batch, N_EXPERTS, nblk, MOE_TBLK),
                  prm["w_e_gate"], prm["w_e_up"], prm["w_e_down"])
    x2 = _moe_combine(offs_flat, x1.reshape(batch, seq, d), ye, pos.transpose(0, 2, 1), p3,
                      prm["w_ple"], prm["w_ple_gate"], prm["g_ple"], prm["g_ple_gate"])
    return x2.reshape(batch * seq, d)


def kernel(x, p, positions, g_mix, w_in, ln_v_g, ln_v_b, w_s, b_s, q_norm_g, k_norm_g, conv_w, a_log, dt_bias,
           o_norm_g, w_out, g_ffn, w_router, w_e_gate, w_e_up, w_e_down, w_ple, g_ple, g_ple_gate, w_ple_gate):
    batch, seq, d = x.shape
    depth = p.shape[0]
    in_width = w_in.shape[-1]
    in_pad = -in_width % LANES
    w_in_b = jnp.pad(w_in, ((0, 0), (0, 0), (0, in_pad))).astype(BF16)
    tables = _rope_tables(positions)
    x2d = x.reshape(batch * seq, d)
    for i in range(depth):
        prm = dict(g_mix=g_mix[i], w_in=w_in_b[i], ln_v_g=ln_v_g[i], ln_v_b=ln_v_b[i], w_s=w_s[i], b_s=b_s[i],
                   q_norm_g=q_norm_g[i], k_norm_g=k_norm_g[i], conv_w=conv_w[i], a_log=a_log[i],
                   dt_bias=dt_bias[i], o_norm_g=o_norm_g[i], w_out=w_out[i].astype(BF16), g_ffn=g_ffn[i],
                   w_router=w_router[i], w_e_gate=w_e_gate[i].astype(BF16), w_e_up=w_e_up[i].astype(BF16),
                   w_e_down=w_e_down[i].astype(BF16), w_ple=w_ple[i].astype(BF16), g_ple=g_ple[i],
                   g_ple_gate=g_ple_gate[i], w_ple_gate=w_ple_gate[i].astype(BF16))
        x2d = _layer(x2d, p[i], tables, batch, seq, prm)
    return x2d.reshape(batch, seq, d)
```

```python
import functools

import jax
import jax.numpy as jnp
from jax import lax
from jax.experimental import pallas as pl
from jax.experimental.pallas import tpu as pltpu

F32 = jnp.float32
BF16 = jnp.bfloat16
I32 = jnp.int32

HEAD_DIM = 64
GMLP_WIDTH = 256
GMLP_GROUPS = 4
GMLP_CHUNK = 128
DSA_WIDTH = 384
DSA_PATTERNS = ((128, 1), (512, 4), (2048, 16))
GDN_WIDTH = 384
GDN_CONV = 5
ROT_DIM = 16
ROPE_THETA = 500000.0
N_EXPERTS = 16
EC_CAPACITY = 2
NORM_EPS = 1e-6
MASK_VALUE = -1e30

LANES = 128
SUBLANES = 8
VMEM_LIMIT_BYTES = 56 * 1024 * 1024

PAIR = 2 * HEAD_DIM
GDN_CHUNK = 64
GDN_GROUP = 4
ATT_HALO = 64
ATT_SUB = 128
OFFS_W = 40
MOE_TBLK = 512


def _cparams(*sem):
    return pltpu.CompilerParams(dimension_semantics=sem, vmem_limit_bytes=VMEM_LIMIT_BYTES)


def _dot(a, b):
    return jnp.dot(a, b, preferred_element_type=F32)


def _dot_nt(a, b):
    return lax.dot_general(a, b, (((1,), (1,)), ((), ())), preferred_element_type=F32)


def _dot_tn(a, b):
    return lax.dot_general(a, b, (((0,), (0,)), ((), ())), preferred_element_type=F32)


def _split2(x):
    hi = x.astype(BF16)
    lo = (x - hi.astype(F32)).astype(BF16)
    return hi, lo


def _split3(x):
    hi = x.astype(BF16)
    r = x - hi.astype(F32)
    mid = r.astype(BF16)
    lo = (r - mid.astype(F32)).astype(BF16)
    return hi, mid, lo


def _group_mean(x, avg):
    hi, lo = _split2(x)
    return _dot(hi, avg) + _dot(lo, avg)


def _rms(x):
    return x * lax.rsqrt(jnp.mean(x * x, axis=-1, keepdims=True) + NORM_EPS)


def _block_avg_matrix(width, group):
    i = jnp.arange(width)
    return jnp.where((i[:, None] // group) == (i[None, :] // group), 1.0 / group, 0.0).astype(BF16)


def _inproj_kernel(x_ref, g_ref, w_ref, *o_refs):
    x = x_ref[...]
    xn = (_rms(x) * g_ref[...]).astype(BF16)
    col = 0
    for o_ref in o_refs:
        n = o_ref.shape[-1]
        o_ref[...] = _dot(xn, w_ref[:, col:col + n])
        col += n


def _inproj(x2d, g, w, widths, tm=512):
    m, d = x2d.shape
    n = w.shape[1]
    return pl.pallas_call(
        _inproj_kernel,
        grid=(m // tm,),
        in_specs=[pl.BlockSpec((tm, d), lambda i: (i, 0)),
                  pl.BlockSpec((1, d), lambda i: (0, 0)),
                  pl.BlockSpec((d, n), lambda i: (0, 0))],
        out_specs=[pl.BlockSpec((tm, wd), lambda i: (i, 0)) for wd in widths],
        out_shape=[jax.ShapeDtypeStruct((m, wd), F32) for wd in widths],
        compiler_params=_cparams("parallel"),
        name="inproj",
    )(x2d, g, w)


def _gmlp_kernel(u_ref, v_ref, lng_ref, lnb_ref, ws_ref, bias_ref, avg_ref, o_ref):
    tm = u_ref.shape[0]
    u = jax.nn.gelu(u_ref[...])
    vf = jax.nn.gelu(v_ref[...])
    avg = avg_ref[...]
    mu = _group_mean(vf, avg)
    dv = vf - mu
    var = _group_mean(dv * dv, avg)
    vn = dv * lax.rsqrt(var + NORM_EPS) * lng_ref[...] + lnb_ref[...]
    grp = lax.broadcasted_iota(I32, (GMLP_CHUNK, GMLP_WIDTH), 1) // HEAD_DIM
    for c in range(tm // GMLP_CHUNK):
        rows = slice(c * GMLP_CHUNK, (c + 1) * GMLP_CHUNK)
        vc = vn[rows].astype(BF16)
        mixed = bias_ref[...]
        for g in range(GMLP_GROUPS):
            mixed = mixed + jnp.where(grp == g, _dot(ws_ref[g], vc), 0.0)
        o_ref[rows, :] = (u[rows] * mixed).astype(BF16)


def _gmlp(pa, ln_g, ln_b, w_s, b_s, tm=512):
    m = pa.shape[0]
    w = GMLP_WIDTH
    bias2d = jnp.repeat(b_s.T, HEAD_DIM, axis=1)
    const = lambda i: (0, 0)
    return pl.pallas_call(
        _gmlp_kernel,
        grid=(m // tm,),
        in_specs=[pl.BlockSpec((tm, w), lambda i: (i, 0)),
                  pl.BlockSpec((tm, w), lambda i: (i, 1)),
                  pl.BlockSpec((1, w), const),
                  pl.BlockSpec((1, w), const),
                  pl.BlockSpec((GMLP_GROUPS, GMLP_CHUNK, GMLP_CHUNK), lambda i: (0, 0, 0)),
                  pl.BlockSpec((GMLP_CHUNK, w), const),
                  pl.BlockSpec((w, w), const)],
        out_specs=pl.BlockSpec((tm, w), lambda i: (i, 0)),
        out_shape=jax.ShapeDtypeStruct((m, w), BF16),
        compiler_params=_cparams("parallel"),
        name="gmlp",
    )(pa, pa, ln_g.reshape(1, w), ln_b.reshape(1, w), w_s.astype(BF16), bias2d,
      _block_avg_matrix(w, HEAD_DIM))


def _rope_table_kernel(pos_ref, invf_ref, sa_ref, sb_ref, cos_ref, sina_ref, sinb_ref):
    ang = pos_ref[...].astype(F32) * invf_ref[...]
    s = jnp.sin(ang)
    cos_ref[...] = jnp.cos(ang)
    sina_ref[...] = s * sa_ref[...]
    sinb_ref[...] = s * sb_ref[...]


def _rope_tables(positions, ts=1024):
    m = positions.size
    half = ROT_DIM // 2
    inv_freq = ROPE_THETA ** (-jnp.arange(half, dtype=F32) * 2.0 / ROT_DIM)
    lane = jnp.arange(PAIR) % HEAD_DIM
    invf = jnp.where(lane < ROT_DIM, inv_freq[lane % half], 0.0).reshape(1, PAIR)
    sa = jnp.where(lane < half, -1.0, 0.0).astype(F32).reshape(1, PAIR)
    sb = jnp.where((lane >= half) & (lane < ROT_DIM), 1.0, 0.0).astype(F32).reshape(1, PAIR)
    const = lambda i: (0, 0)
    row = pl.BlockSpec((ts, PAIR), lambda i: (i, 0))
    return pl.pallas_call(
        _rope_table_kernel,
        grid=(m // ts,),
        in_specs=[pl.BlockSpec((ts, 1), lambda i: (i, 0)),
                  pl.BlockSpec((1, PAIR), const), pl.BlockSpec((1, PAIR), const),
                  pl.BlockSpec((1, PAIR), const)],
        out_specs=[row, row, row],
        out_shape=[jax.ShapeDtypeStruct((m, PAIR), F32)] * 3,
        compiler_params=_cparams("parallel"),
        name="rope_tables",
    )(positions.reshape(m, 1), invf, sa, sb)


def _dsa_prep_kernel(q_ref, k_ref, v_ref, cos_ref, sina_ref, sinb_ref, gq_ref, gk_ref, avg_ref, *refs):
    o_refs, stage = refs[:-1], refs[-1]
    tm = q_ref.shape[0]
    w = DSA_WIDTH
    cos, sina, sinb = cos_ref[...], sina_ref[...], sinb_ref[...]
    avg = avg_ref[...]
    half = ROT_DIM // 2

    def norm_rot(t, g, scale, col0):
        t = t * lax.rsqrt(_group_mean(t * t, avg) + NORM_EPS) * g
        for hp in range(w // PAIR):
            tp = t[:, hp * PAIR:(hp + 1) * PAIR]
            rot = tp * cos + pltpu.roll(tp, PAIR - half, 1) * sina + pltpu.roll(tp, half, 1) * sinb
            stage[col0 + hp] = rot * scale

    npairs = w // PAIR
    norm_rot(q_ref[...], gq_ref[...], HEAD_DIM ** -0.5, 0)
    norm_rot(k_ref[...], gk_ref[...], 1.0, npairs)
    for hp in range(npairs):
        stage[2 * npairs + hp] = v_ref[:, hp * PAIR:(hp + 1) * PAIR]
    for o_ref, (_, dil) in zip(o_refs, DSA_PATTERNS):
        for r in range(dil):
            rows = pl.ds(r, tm // dil, stride=dil) if dil > 1 else slice(None)
            for g in range(3 * npairs):
                o_ref[r, :, g * PAIR:(g + 1) * PAIR] = stage[g, rows, :].astype(BF16)


def _dsa_prep(pb, tables, batch, seq, q_norm_g, k_norm_g, tm=512):
    w = DSA_WIDTH
    const = lambda b, i: (0, 0)
    pb3 = pb.reshape(batch, seq, 3 * w)
    tab = pl.BlockSpec((None, tm, PAIR), lambda b, i: (b, i, 0))
    gq = jnp.tile(q_norm_g, w // HEAD_DIM).reshape(1, w)
    gk = jnp.tile(k_norm_g, w // HEAD_DIM).reshape(1, w)
    dils = [dil for _, dil in DSA_PATTERNS]
    return pl.pallas_call(
        _dsa_prep_kernel,
        grid=(batch, seq // tm),
        in_specs=[pl.BlockSpec((None, tm, w), lambda b, i: (b, i, 0)),
                  pl.BlockSpec((None, tm, w), lambda b, i: (b, i, 1)),
                  pl.BlockSpec((None, tm, w), lambda b, i: (b, i, 2)),
                  tab, tab, tab,
                  pl.BlockSpec((1, w), const), pl.BlockSpec((1, w), const),
                  pl.BlockSpec((w, w), const)],
        out_specs=[pl.BlockSpec((None, dil, tm // dil, 3 * w), lambda b, i: (b, 0, i, 0)) for dil in dils],
        out_shape=[jax.ShapeDtypeStruct((batch, dil, seq // dil, 3 * w), BF16) for dil in dils],
        scratch_shapes=[pltpu.VMEM((3 * w // PAIR, tm, PAIR), F32)],
        compiler_params=_cparams("parallel", "parallel"),
        name="dsa_prep",
    )(pb3, pb3, pb3, *[t.reshape(batch, seq, PAIR) for t in tables], gq, gk, _block_avg_matrix(w, HEAD_DIM))


def _dsa_kernel(q_ref, kp_ref, k_ref, kn_ref, vp_ref, v_ref, vn_ref, o_ref, lse_ref, kbuf, vtbuf, *,
                steps, sub_len):
    t = q_ref.shape[0]
    i = pl.program_id(2)
    kbuf[0:ATT_HALO, :] = kp_ref[...]
    kbuf[ATT_HALO:ATT_HALO + t, :] = k_ref[...]
    kbuf[ATT_HALO + t:, :] = kn_ref[...]
    npairs = DSA_WIDTH // PAIR
    vfull = jnp.concatenate([vp_ref[...], v_ref[...], vn_ref[...]], axis=0).astype(F32)
    for hp in range(npairs):
        vtbuf[hp] = vfull[:, hp * PAIR:(hp + 1) * PAIR].T.astype(BF16)

    nk = ATT_SUB + 2 * ATT_HALO
    lane = lax.broadcasted_iota(I32, (1, PAIR), 1)
    lane_mask = (lane < HEAD_DIM, lane >= HEAD_DIM)
    row = lax.broadcasted_iota(I32, (PAIR, 1), 0)
    row_mask = (row < HEAD_DIM, row >= HEAD_DIM)
    stat_row = lax.broadcasted_iota(I32, (LANES, ATT_SUB), 0)
    for s in range(t // ATT_SUB):
        rows = slice(s * ATT_SUB, (s + 1) * ATT_SUB)
        base = i * t + s * ATT_SUB
        kj = base - ATT_HALO + lax.broadcasted_iota(I32, (nk, 1), 0)
        qi = base + lax.broadcasted_iota(I32, (1, ATT_SUB), 1)
        valid = (jnp.abs(kj - qi) <= steps) & (kj >= 0) & (kj < sub_len)
        kk = kbuf[s * ATT_SUB:s * ATT_SUB + nk, :]
        stats = jnp.zeros((LANES, ATT_SUB), F32)
        for hp in range(npairs):
            cols = slice(hp * PAIR, (hp + 1) * PAIR)
            qp = q_ref[rows, cols]
            kp = kk[:, cols]
            vt = vtbuf[hp, :, s * ATT_SUB:s * ATT_SUB + nk]
            acc_t = jnp.zeros((PAIR, ATT_SUB), F32)
            for hh in range(2):
                km = jnp.where(lane_mask[hh], kp, jnp.zeros_like(kp))
                sc = jnp.where(valid, _dot_nt(km, qp), MASK_VALUE)
                m = jnp.max(sc, axis=0, keepdims=True)
                p = jnp.exp(sc - m)
                l = jnp.sum(p, axis=0, keepdims=True)
                vth = jnp.where(row_mask[hh], vt, jnp.zeros_like(vt))
                acc_t = acc_t + _dot(vth, p.astype(BF16)) * (1.0 / l)
                stats = jnp.where(stat_row == 2 * hp + hh, m + jnp.log(l), stats)
            o_ref[rows, cols] = acc_t.T.astype(BF16)
        lse_ref[rows, :] = stats.T


def _dsa_pattern(qkv, window, dil):
    batch, _, sub_len, _ = qkv.shape
    w = DSA_WIDTH
    steps = window // (2 * dil)
    assert steps <= ATT_HALO
    t = min(512, sub_len)
    nb64 = sub_len // ATT_HALO
    r64 = t // ATT_HALO

    def main(which):
        return pl.BlockSpec((None, None, t, w), lambda b, r, i: (b, r, i, which))

    def prev(which):
        return pl.BlockSpec((None, None, ATT_HALO, w),
                            lambda b, r, i: (b, r, jnp.maximum(i * r64 - 1, 0), which))

    def nxt(which):
        return pl.BlockSpec((None, None, ATT_HALO, w),
                            lambda b, r, i: (b, r, jnp.minimum((i + 1) * r64, nb64 - 1), which))

    out = lambda wd: pl.BlockSpec((None, None, t, wd), lambda b, r, i: (b, r, i, 0))
    return pl.pallas_call(
        functools.partial(_dsa_kernel, steps=steps, sub_len=sub_len),
        grid=(batch, dil, sub_len // t),
        in_specs=[main(0), prev(1), main(1), nxt(1), prev(2), main(2), nxt(2)],
        out_specs=[out(w), out(LANES)],
        out_shape=[jax.ShapeDtypeStruct((batch, dil, sub_len, w), BF16),
                   jax.ShapeDtypeStruct((batch, dil, sub_len, LANES), F32)],
        scratch_shapes=[pltpu.VMEM((t + 2 * ATT_HALO, w), BF16),
                        pltpu.VMEM((w // PAIR, PAIR, t + 2 * ATT_HALO), BF16)],
        compiler_params=_cparams("parallel", "parallel", "parallel"),
        name=f"dsa_d{dil}",
    )(*([qkv] * 7))


def _dsa_combine_kernel(*refs):
    n = len(DSA_PATTERNS)
    o_refs, lse_refs, expand_ref, y_ref = refs[:n], refs[n:2 * n], refs[2 * n], refs[2 * n + 1]
    stages = iter(refs[2 * n + 2:])
    tm = y_ref.shape[0]
    npairs = DSA_WIDTH // PAIR
    outs, lses = [], []
    for o_ref, lse_ref, (_, dil) in zip(o_refs, lse_refs, DSA_PATTERNS):
        if dil == 1:
            outs.append([o_ref[0, :, hp * PAIR:(hp + 1) * PAIR].astype(F32) for hp in range(npairs)])
            lses.append(lse_ref[0])
        else:
            so, sl = next(stages), next(stages)
            for r in range(dil):
                rows = pl.ds(r, tm // dil, stride=dil)
                for hp in range(npairs):
                    so[hp, rows, :] = o_ref[r, :, hp * PAIR:(hp + 1) * PAIR].astype(F32)
                sl[rows, :] = lse_ref[r]
            outs.append([so[hp] for hp in range(npairs)])
            lses.append(sl[...])
    top = functools.reduce(jnp.maximum, lses)
    es = [jnp.exp(lse - top) for lse in lses]
    inv = 1.0 / sum(es)
    weights = [_group_mean(e * inv, expand_ref[...]) for e in es]
    for hp in range(npairs):
        cols = slice(hp * PAIR, (hp + 1) * PAIR)
        y_ref[:, cols] = sum(o[hp] * wgt[:, cols] for o, wgt in zip(outs, weights)).astype(BF16)


def _dsa_combine(outs, lses, batch, seq, tm=512):
    w = DSA_WIDTH
    dils = [dil for _, dil in DSA_PATTERNS]
    lane = jnp.arange(w) // HEAD_DIM
    expand = (jnp.arange(LANES)[:, None] == lane[None, :]).astype(BF16)
    spec = lambda dil, wd: pl.BlockSpec((None, dil, tm // dil, wd), lambda b, i: (b, 0, i, 0))
    scratch = []
    for dil in dils:
        if dil > 1:
            scratch += [pltpu.VMEM((w // PAIR, tm, PAIR), F32), pltpu.VMEM((tm, LANES), F32)]
    return pl.pallas_call(
        _dsa_combine_kernel,
        grid=(batch, seq // tm),
        in_specs=[spec(dil, w) for dil in dils] + [spec(dil, LANES) for dil in dils]
                 + [pl.BlockSpec((LANES, w), lambda b, i: (0, 0))],
        out_specs=pl.BlockSpec((None, tm, w), lambda b, i: (b, i, 0)),
        out_shape=jax.ShapeDtypeStruct((batch, seq, w), BF16),
        scratch_shapes=scratch,
        compiler_params=_cparams("parallel", "parallel"),
        name="dsa_combine",
    )(*outs, *lses, expand).reshape(batch * seq, w)


def _dsa(qkvs, batch, seq):
    res = [_dsa_pattern(qkv, window, dil) for qkv, (window, dil) in zip(qkvs, DSA_PATTERNS)]
    return _dsa_combine([r[0] for r in res], [r[1] for r in res], batch, seq)


def _gdn_kernel(xq_ref, xk_ref, xv_ref, gate_ref, ab_ref, cw_ref, alog_ref, dtb_ref, on_ref,
                ones_ref, avg_ref, o_ref, m_s, n_s, qp_s, op_s, el_s):
    seq = xq_ref.shape[0]
    c = GDN_CHUNK
    nch = seq // c
    halo = SUBLANES
    lane = lax.broadcasted_iota(I32, (1, PAIR), 1)
    head_mask = (lane < HEAD_DIM, lane >= HEAD_DIM)
    ti = lax.broadcasted_iota(I32, (c, c), 0)
    si = lax.broadcasted_iota(I32, (c, c), 1)
    incl = (si <= ti, si >= ti)
    strict = (si < ti, si > ti)
    tri_incl = tuple(jnp.where(m, 1.0, 0.0).astype(BF16) for m in incl)
    bd_ones = ones_ref[...]
    lane8 = lax.broadcasted_iota(I32, (1, LANES), 1)
    neg_a = -jnp.exp(alog_ref[...])
    dtb = dtb_ref[...]

    def conv_silu(x_ref, which, r0, ci):
        main = x_ref[pl.ds(r0, c), :]
        before = x_ref[pl.ds(jnp.maximum(r0 - halo, 0), halo), :]
        after = x_ref[pl.ds(jnp.minimum(r0 + c, seq - halo), halo), :]
        before = jnp.where(ci > 0, before, 0.0)
        after = jnp.where(ci < nch - 1, after, 0.0)
        win = jnp.concatenate([before, main, after], axis=0)
        n = c + 2 * halo
        y = jnp.zeros((c, PAIR), F32)
        for j in range(GDN_CONV):
            shifted = win if j == GDN_CONV // 2 else pltpu.roll(win, (GDN_CONV // 2 - j) % n, 0)
            y = y + shifted[halo:halo + c] * cw_ref[which, j:j + 1, :]
        return y * jax.nn.sigmoid(y)

    def l2n(x):
        hi, lo = _split2(x * x)
        ss = _dot(hi, bd_ones) + _dot(lo, bd_ones)
        return x * lax.rsqrt(ss + NORM_EPS)

    eye = jnp.where(si == ti, 1.0, 0.0)
    li = lax.broadcasted_iota(I32, (PAIR, PAIR), 0) // HEAD_DIM
    lj = lax.broadcasted_iota(I32, (PAIR, PAIR), 1) // HEAD_DIM
    same_head = li == lj

    def transform(it, carry):
        dirs = []
        for gi in range(GDN_GROUP):
            ci = it * GDN_GROUP + gi
            r0 = pl.multiple_of(ci * c, c)
            q = l2n(conv_silu(xq_ref, 0, r0, ci)) * (HEAD_DIM ** -0.5)
            k = l2n(conv_silu(xk_ref, 1, r0, ci))
            v = conv_silu(xv_ref, 2, r0, ci)
            ab = ab_ref[pl.ds(r0, c), :]
            gb = jnp.where(lane8 < 4, neg_a * jax.nn.softplus(ab + dtb), jax.nn.sigmoid(ab))
            g3 = _split3(gb)
            for d in range(2):
                dirs.append(dict(ci=ci, r0=r0, d=d, q=q, k=k, v=v, gb=gb, g3=g3))
        for dd in dirs:
            dd["gc"] = sum(_dot(tri_incl[dd["d"]], part) for part in dd["g3"])
        chains = []
        for dd in dirs:
            d, gc, gb, k = dd["d"], dd["gc"], dd["gb"], dd["k"]
            gct = gc.T
            g_last = gc[c - 1:c, :] if d == 0 else gc[0:1, :]
            dd["chains"] = []
            for hh in range(2):
                col = 2 * d + hh
                gc_c = gc[:, col:col + 1]
                beta = gb[:, 4 + col:5 + col]
                km = jnp.where(head_mask[hh], k, 0.0)
                kb = km * beta
                eg = jnp.exp(gc_c)
                gl = g_last[:, col:col + 1]
                ch = dict(d=d,
                          decay=jnp.where(incl[d], jnp.exp(jnp.minimum(gc_c - gct[col:col + 1, :], 0.0)), 0.0),
                          kmb=km.astype(BF16), kbb=kb.astype(BF16),
                          qmb=jnp.where(head_mask[hh], dd["q"], 0.0).astype(BF16),
                          x0=jnp.concatenate([jnp.where(head_mask[hh], dd["v"], 0.0) * beta, kb * eg],
                                             axis=1).astype(BF16),
                          eg=eg, erem=jnp.exp(gl - gc_c), elast=jnp.exp(gl))
                chains.append(ch)
                dd["chains"].append(ch)
        for ch in chains:
            ch["a"] = _dot_nt(ch["kbb"], ch["kmb"])
        for ch in chains:
            ch["qkr"] = _dot_nt(ch["qmb"], ch["kmb"])
        for ch in chains:
            lmat = jnp.where(strict[ch["d"]], ch["a"] * ch["decay"], 0.0)
            ch["pw"] = lmat.astype(BF16)
            ch["t"] = eye - lmat
            ch["qk"] = jnp.where(incl[ch["d"]], ch["qkr"] * ch["decay"], 0.0).astype(BF16)
        for ch in chains:
            ch["sq"] = _dot(ch["pw"], ch["pw"])
        for ch in chains:
            ch["pw"] = ch["sq"].astype(BF16)
        for _ in range(4):
            for ch in chains:
                ch["tp"] = _dot(ch["t"].astype(BF16), ch["pw"])
            for ch in chains:
                ch["sq"] = _dot(ch["pw"], ch["pw"])
            for ch in chains:
                ch["t"] = ch["t"] + ch["tp"]
                ch["pw"] = ch["sq"].astype(BF16)
        for ch in chains:
            ch["tp"] = _dot(ch["t"].astype(BF16), ch["pw"])
        for ch in chains:
            ch["xu"] = _dot((ch["t"] + ch["tp"]).astype(BF16), ch["x0"])
        for dd in dirs:
            d, r0, ci = dd["d"], dd["r0"], dd["ci"]
            c0, c1 = dd["chains"]
            u_h = [c0["xu"][:, :PAIR].astype(BF16), c1["xu"][:, :PAIR].astype(BF16)]
            w_h = [c0["xu"][:, PAIR:].astype(BF16), c1["xu"][:, PAIR:].astype(BF16)]
            kd = (dd["k"] * jnp.where(head_mask[0], c0["erem"], c1["erem"])).astype(BF16)
            m_s[d, ci] = jnp.where(same_head, _dot_tn(kd, w_h[0] + w_h[1]), 0.0).astype(BF16)
            n_s[d, ci] = jnp.where(same_head, _dot_tn(kd, u_h[0] + u_h[1]), 0.0)
            qg = dd["q"] * jnp.where(head_mask[0], c0["eg"], c1["eg"])
            qp_s[d, pl.ds(r0, c), :] = (qg - _dot(c0["qk"], w_h[0]) - _dot(c1["qk"], w_h[1])).astype(BF16)
            op_s[d, pl.ds(r0, c), :] = _dot(c0["qk"], u_h[0]) + _dot(c1["qk"], u_h[1])
            el_s[d, pl.ds(ci, 1), :] = jnp.where(head_mask[0], c0["elast"], c1["elast"])
        return carry

    lax.fori_loop(0, nch // GDN_GROUP, transform, 0)

    def scan(i, states):
        cis = (i, nch - 1 - i)
        rows = [pl.ds(pl.multiple_of(ci * c, c), c) for ci in cis]
        stb = [st.astype(BF16) for st in states]
        trans = [_dot(m_s[d, cis[d]], stb[d]) for d in range(2)]
        outs = [_dot(qp_s[d, rows[d], :], stb[d]) for d in range(2)]
        new_states = []
        for d in range(2):
            op_s[d, rows[d], :] += outs[d]
            new_states.append(states[d] * el_s[d, pl.ds(cis[d], 1), :] - trans[d] + n_s[d, cis[d]])
        return tuple(new_states)

    zero = jnp.zeros((PAIR, PAIR), F32)
    lax.fori_loop(0, nch, scan, (zero, zero))

    avg = avg_ref[...]
    tile = 512

    def finish(ti_, carry):
        r0 = pl.multiple_of(ti_ * tile, tile)
        o = op_s[0, pl.ds(r0, tile), :] + op_s[1, pl.ds(r0, tile), :]
        o = o * lax.rsqrt(_group_mean(o * o, avg) + NORM_EPS) * on_ref[...]
        gate = gate_ref[pl.ds(r0, tile), :]
        o_ref[pl.ds(r0, tile), :] = (o * (gate * jax.nn.sigmoid(gate))).astype(BF16)
        return carry

    lax.fori_loop(0, seq // tile, finish, 0)


def _gdn(pc, batch, seq, conv_w, a_log, dt_bias, o_norm_g):
    w = GDN_WIDTH
    heads = w // HEAD_DIM
    pairs = w // PAIR
    pc3 = pc.reshape(batch, seq, pc.shape[-1])
    ab = pc3[:, :, 4 * w:4 * w + 4 * heads].reshape(batch, seq, 2, 2, pairs, 2)
    ab = ab.transpose(0, 4, 1, 2, 3, 5).reshape(batch, pairs, seq, 8)
    ab = jnp.pad(ab, ((0, 0), (0, 0), (0, 0), (0, LANES - 8)))
    cw = conv_w.reshape(GDN_CONV, 3, w).transpose(1, 0, 2)
    cw = jnp.pad(cw, ((0, 0), (0, SUBLANES - GDN_CONV), (0, 0)))

    def pair_lanes(p):
        x = p.reshape(2, pairs, 2).transpose(1, 0, 2).reshape(pairs, 1, 4)
        return jnp.pad(x, ((0, 0), (0, 0), (0, LANES - 4)))

    nblk = w // PAIR

    def col(offset):
        return pl.BlockSpec((None, seq, PAIR), lambda b, hp: (b, 0, offset + hp))

    nch = seq // GDN_CHUNK
    return pl.pallas_call(
        _gdn_kernel,
        grid=(batch, pairs),
        in_specs=[col(0), col(nblk), col(2 * nblk), col(3 * nblk),
                  pl.BlockSpec((None, None, seq, LANES), lambda b, hp: (b, hp, 0, 0)),
                  pl.BlockSpec((3, SUBLANES, PAIR), lambda b, hp: (0, 0, hp)),
                  pl.BlockSpec((None, 1, LANES), lambda b, hp: (hp, 0, 0)),
                  pl.BlockSpec((None, 1, LANES), lambda b, hp: (hp, 0, 0)),
                  pl.BlockSpec((1, PAIR), lambda b, hp: (0, 0)),
                  pl.BlockSpec((PAIR, PAIR), lambda b, hp: (0, 0)),
                  pl.BlockSpec((PAIR, PAIR), lambda b, hp: (0, 0))],
        out_specs=pl.BlockSpec((None, seq, PAIR), lambda b, hp: (b, 0, hp)),
        out_shape=jax.ShapeDtypeStruct((batch, seq, w), BF16),
        scratch_shapes=[pltpu.VMEM((2, nch, PAIR, PAIR), BF16),
                        pltpu.VMEM((2, nch, PAIR, PAIR), F32),
                        pltpu.VMEM((2, seq, PAIR), BF16),
                        pltpu.VMEM((2, seq, PAIR), F32),
                        pltpu.VMEM((2, nch, PAIR), F32)],
        compiler_params=_cparams("parallel", "parallel"),
        name="gdn",
    )(pc3, pc3, pc3, pc3, ab, cw, pair_lanes(a_log), pair_lanes(dt_bias),
      jnp.tile(o_norm_g, 2).reshape(1, PAIR),
      (_block_avg_matrix(PAIR, HEAD_DIM) * HEAD_DIM).astype(BF16),
      _block_avg_matrix(PAIR, HEAD_DIM)).reshape(batch * seq, w)


def _outproj_kernel(x_ref, ya_ref, yb_ref, yc_ref, wa_ref, wb_ref, wc_ref, g_ref, x1_ref, h_ref):
    x1 = (x_ref[...] + _dot(ya_ref[...], wa_ref[...]) + _dot(yb_ref[...], wb_ref[...])
          + _dot(yc_ref[...], wc_ref[...]))
    x1_ref[...] = x1
    h_ref[...] = (_rms(x1) * g_ref[...]).astype(BF16)


def _outproj(x2d, ya, yb, yc, w_out, g_ffn, tm=512):
    m, d = x2d.shape
    wa = w_out[:GMLP_WIDTH]
    wb = w_out[GMLP_WIDTH:GMLP_WIDTH + DSA_WIDTH]
    wc = w_out[GMLP_WIDTH + DSA_WIDTH:]
    const = lambda i: (0, 0)
    row = lambda wd: pl.BlockSpec((tm, wd), lambda i: (i, 0))
    return pl.pallas_call(
        _outproj_kernel,
        grid=(m // tm,),
        in_specs=[row(d), row(GMLP_WIDTH), row(DSA_WIDTH), row(GDN_WIDTH),
                  pl.BlockSpec(wa.shape, const), pl.BlockSpec(wb.shape, const),
                  pl.BlockSpec(wc.shape, const), pl.BlockSpec((1, d), const)],
        out_specs=[row(d), row(d)],
        out_shape=[jax.ShapeDtypeStruct((m, d), F32), jax.ShapeDtypeStruct((m, d), BF16)],
        compiler_params=_cparams("parallel"),
        name="outproj",
    )(x2d, ya, yb, yc, wa, wb, wc, g_ffn.reshape(1, d))


def _router_kernel(h_ref, wr_ref, tri_ref, pos_ref, aff_ref, offs_ref, bits_s, *, cap):
    seq = h_ref.shape[0]
    nblk = seq // LANES
    logits = _dot_nt(wr_ref[...], h_ref[...])
    ex = jnp.exp(logits - jnp.max(logits, axis=0, keepdims=True))
    aff = ex / jnp.sum(ex, axis=0, keepdims=True)
    aff_ref[...] = aff
    bits = lax.bitcast_convert_type(aff, I32)
    bits_s[...] = bits

    def bisect(it, prefix):
        cand = prefix | jnp.left_shift(jnp.int32(1), 30 - it)
        cnt = jnp.sum(jnp.where(bits_s[...] >= cand, 1.0, 0.0), axis=1, keepdims=True)
        return jnp.where(cnt >= cap, cand, prefix)

    thr = lax.fori_loop(0, 31, bisect, jnp.zeros((N_EXPERTS, 1), I32))
    n_gt = jnp.sum(jnp.where(bits > thr, 1.0, 0.0), axis=1, keepdims=True)
    need = cap - n_gt
    tri = tri_ref[...]
    lane = lax.broadcasted_iota(I32, (N_EXPERTS, LANES), 1)

    off_eq = jnp.zeros((N_EXPERTS, 1), F32)
    off_sel = jnp.zeros((N_EXPERTS, 1), F32)
    offs = jnp.zeros((N_EXPERTS, LANES), F32)
    for j in range(nblk):
        cols = slice(j * LANES, (j + 1) * LANES)
        bj = bits_s[:, cols]
        eq = bj == thr
        eq_f = jnp.where(eq, 1.0, 0.0)
        rank = off_eq + _dot(eq_f.astype(BF16), tri)
        sel = (bj > thr) | (eq & (rank < need))
        sel_f = jnp.where(sel, 1.0, 0.0)
        pos = off_sel + _dot(sel_f.astype(BF16), tri)
        pos_ref[:, cols] = jnp.where(sel, pos, -1.0)
        offs = jnp.where(lane == j, off_sel, offs)
        off_eq = off_eq + jnp.sum(eq_f, axis=1, keepdims=True)
        off_sel = off_sel + jnp.sum(sel_f, axis=1, keepdims=True)
    offs_ref[...] = jnp.where(lane == nblk, off_sel, offs).astype(I32)


def _router(h3, w_router):
    batch, seq, d = h3.shape
    cap = EC_CAPACITY * seq // N_EXPERTS
    i = jnp.arange(LANES)
    tri = jnp.where(i[:, None] < i[None, :], 1.0, 0.0).astype(BF16)
    out3 = lambda wd: pl.BlockSpec((None, N_EXPERTS, wd), lambda b: (b, 0, 0))
    return pl.pallas_call(
        functools.partial(_router_kernel, cap=cap),
        grid=(batch,),
        in_specs=[pl.BlockSpec((None, seq, d), lambda b: (b, 0, 0)),
                  pl.BlockSpec((N_EXPERTS, d), lambda b: (0, 0)),
                  pl.BlockSpec((LANES, LANES), lambda b: (0, 0))],
        out_specs=[out3(seq), out3(seq), out3(LANES)],
        out_shape=[jax.ShapeDtypeStruct((batch, N_EXPERTS, seq), F32),
                   jax.ShapeDtypeStruct((batch, N_EXPERTS, seq), F32),
                   jax.ShapeDtypeStruct((batch, N_EXPERTS, LANES), I32)],
        scratch_shapes=[pltpu.VMEM((N_EXPERTS, seq), I32)],
        compiler_params=_cparams("parallel"),
        name="router",
    )(h3, w_router.T.astype(BF16), tri)


def _moe_ffn_kernel(offs_ref, h_ref, pos_ref, aff_ref, wg_ref, wu_ref, wd_ref, ye_ref, x_s, gate_s, *, cap):
    nblk, tblk = pos_ref.shape
    per = tblk // LANES
    e = pl.program_id(0)
    b = pl.program_id(1)
    base = (b * N_EXPERTS + e) * OFFS_W
    x_s[...] = jnp.zeros_like(x_s)
    gate_s[...] = jnp.zeros_like(gate_s)
    rid = lax.broadcasted_iota(I32, (LANES, tblk), 0)
    for c in range(cap // LANES):
        rows = slice(c * LANES, (c + 1) * LANES)
        want = (rid + c * LANES).astype(F32)

        def gather(j, carry):
            lo = offs_ref[base + j * per]
            hi = offs_ref[base + (j + 1) * per]

            @pl.when((lo < (c + 1) * LANES) & (hi > c * LANES))
            def _():
                hit = pos_ref[pl.ds(j, 1), :] == want
                hj = h_ref[pl.ds(pl.multiple_of(j * tblk, tblk), tblk), :]
                x_s[rows, :] += _dot(jnp.where(hit, 1.0, 0.0).astype(BF16), hj)
                gate_s[rows, :] += jnp.sum(jnp.where(hit, aff_ref[pl.ds(j, 1), :], 0.0), axis=1, keepdims=True)

            return carry

        lax.fori_loop(0, nblk, gather, 0)

    half = cap // 2
    for r in range(2):
        rows = slice(r * half, (r + 1) * half)
        x = x_s[rows, :].astype(BF16)
        g = _dot(x, wg_ref[...])
        u = _dot(x, wu_ref[...])
        hid = (g * jax.nn.sigmoid(g) * u).astype(BF16)
        ye_ref[rows, :] = (_dot(hid, wd_ref[...]) * gate_s[rows, 0:1]).astype(BF16)


def _moe_ffn(offs_flat, h3, pos4, aff4, wg, wu, wd):
    batch, seq, d = h3.shape
    cap = EC_CAPACITY * seq // N_EXPERTS
    nblk, tblk = pos4.shape[2:]
    row = pl.BlockSpec((None, None, nblk, tblk), lambda e, b, o: (b, e, 0, 0))
    wspec = pl.BlockSpec((None, d, d), lambda e, b, o: (e, 0, 0))
    return pl.pallas_call(
        functools.partial(_moe_ffn_kernel, cap=cap),
        grid_spec=pltpu.PrefetchScalarGridSpec(
            num_scalar_prefetch=1,
            grid=(N_EXPERTS, batch),
            in_specs=[pl.BlockSpec((None, seq, d), lambda e, b, o: (b, 0, 0)), row, row,
                      wspec, wspec, wspec],
            out_specs=pl.BlockSpec((None, None, cap, d), lambda e, b, o: (b, e, 0, 0)),
            scratch_shapes=[pltpu.VMEM((cap, d), F32), pltpu.VMEM((cap, LANES), F32)]),
        out_shape=jax.ShapeDtypeStruct((batch, N_EXPERTS, cap, d), BF16),
        compiler_params=_cparams("arbitrary", "arbitrary"),
        name="moe_ffn",
    )(offs_flat, h3, pos4, aff4, wg, wu, wd)


def _moe_combine_kernel(offs_ref, x1_ref, ye_ref, ptok_ref, p_ref, wple_ref, wpg_ref, gple_ref, gpg_ref,
                        o_ref, acc_s, *, cap):
    tb = x1_ref.shape[0]
    b = pl.program_id(0)
    jb = pl.program_id(1)
    win = 2 * LANES
    cid = lax.broadcasted_iota(I32, (LANES, win), 1)
    for sub in range(tb // LANES):
        rows = slice(sub * LANES, (sub + 1) * LANES)
        j = jb * (tb // LANES) + sub
        acc = x1_ref[rows, :]
        for e in range(N_EXPERTS):
            lo = offs_ref[(b * N_EXPERTS + e) * OFFS_W + j]
            start = jnp.minimum(lax.shift_right_logical(lo, 7) * LANES, cap - win)
            start = pl.multiple_of(start, LANES)
            hit = ptok_ref[rows, e:e + 1] == (cid + start).astype(F32)
            acc = acc + _dot(jnp.where(hit, 1.0, 0.0).astype(BF16), ye_ref[e, pl.ds(start, win), :])
        acc_s[rows, :] = acc

    x2 = acc_s[...]
    emb = _rms(_dot(p_ref[...].astype(BF16), wple_ref[...])) * gple_ref[...]
    gate = jax.nn.sigmoid(_dot((_rms(x2) * gpg_ref[...]).astype(BF16), wpg_ref[...]))
    o_ref[...] = x2 + emb * gate


def _moe_combine(offs_flat, x1_3, ye, ptok, p3, w_ple, w_pg, g_ple, g_pg, tb=256):
    batch, seq, d = x1_3.shape
    cap = ye.shape[2]
    pd = p3.shape[-1]
    const = lambda b, j, o: (0, 0)
    row = lambda wd: pl.BlockSpec((None, tb, wd), lambda b, j, o: (b, j, 0))
    return pl.pallas_call(
        functools.partial(_moe_combine_kernel, cap=cap),
        grid_spec=pltpu.PrefetchScalarGridSpec(
            num_scalar_prefetch=1,
            grid=(batch, seq // tb),
            in_specs=[row(d),
                      pl.BlockSpec((None, N_EXPERTS, cap, d), lambda b, j, o: (b, 0, 0, 0)),
                      row(N_EXPERTS), row(pd),
                      pl.BlockSpec((pd, d), const), pl.BlockSpec((d, d), const),
                      pl.BlockSpec((1, d), const), pl.BlockSpec((1, d), const)],
            out_specs=row(d),
            scratch_shapes=[pltpu.VMEM((tb, d), F32)]),
        out_shape=jax.ShapeDtypeStruct((batch, seq, d), F32),
        compiler_params=_cparams("arbitrary", "arbitrary"),
        name="moe_combine_ple",
    )(offs_flat, x1_3, ye, ptok, p3, w_ple, w_pg, g_ple.reshape(1, d), g_pg.reshape(1, d))


def _layer(x2d, p3, tables, batch, seq, prm):
    d = x2d.shape[-1]
    pa, pb, pc = _inproj(x2d, prm["g_mix"].reshape(1, d), prm["w_in"],
                         (2 * GMLP_WIDTH, 3 * DSA_WIDTH, prm["w_in"].shape[1] - 2 * GMLP_WIDTH - 3 * DSA_WIDTH))
    ya = _gmlp(pa, prm["ln_v_g"], prm["ln_v_b"], prm["w_s"], prm["b_s"])
    yb = _dsa(_dsa_prep(pb, tables, batch, seq, prm["q_norm_g"], prm["k_norm_g"]), batch, seq)
    yc = _gdn(pc, batch, seq, prm["conv_w"], prm["a_log"], prm["dt_bias"], prm["o_norm_g"])
    x1, h = _outproj(x2d, ya, yb, yc, prm["w_out"], prm["g_ffn"])
    h3 = h.reshape(batch, seq, d)
    pos, aff, offs = _router(h3, prm["w_router"])
    offs_flat = offs[:, :, :OFFS_W].reshape(-1)
    nblk = seq // MOE_TBLK
    ye = _moe_ffn(offs_flat, h3, pos.reshape(batch, N_EXPERTS, nblk, MOE_TBLK),
                  aff.reshape(batch, N_EXPERTS, nblk, MOE_TBLK),
                  prm["w_e_gate"], prm["w_e_up"], prm["w_e_down"])
    x2 = _moe_combine(offs_flat, x1.reshape(batch, seq, d), ye, pos.transpose(0, 2, 1), p3,
                      prm["w_ple"], prm["w_ple_gate"], prm["g_ple"], prm["g_ple_gate"])
    return x2.reshape(batch * seq, d)


def kernel(x, p, positions, g_mix, w_in, ln_v_g, ln_v_b, w_s, b_s, q_norm_g, k_norm_g, conv_w, a_log, dt_bias,
           o_norm_g, w_out, g_ffn, w_router, w_e_gate, w_e_up, w_e_down, w_ple, g_ple, g_ple_gate, w_ple_gate):
    batch, seq, d = x.shape
    depth = p.shape[0]
    in_width = w_in.shape[-1]
    in_pad = -in_width % LANES
    w_in_b = jnp.pad(w_in, ((0, 0), (0, 0), (0, in_pad))).astype(BF16)
    tables = _rope_tables(positions)
    x2d = x.reshape(batch * seq, d)
    for i in range(depth):
        prm = dict(g_mix=g_mix[i], w_in=w_in_b[i], ln_v_g=ln_v_g[i], ln_v_b=ln_v_b[i], w_s=w_s[i], b_s=b_s[i],
                   q_norm_g=q_norm_g[i], k_norm_g=k_norm_g[i], conv_w=conv_w[i], a_log=a_log[i],
                   dt_bias=dt_bias[i], o_norm_g=o_norm_g[i], w_out=w_out[i].astype(BF16), g_ffn=g_ffn[i],
                   w_router=w_router[i], w_e_gate=w_e_gate[i].astype(BF16), w_e_up=w_e_up[i].astype(BF16),
                   w_e_down=w_e_down[i].astype(BF16), w_ple=w_ple[i].astype(BF16), g_ple=g_ple[i],
                   g_ple_gate=g_ple_gate[i], w_ple_gate=w_ple_gate[i].astype(BF16))
        x2d = _layer(x2d, p[i], tables, batch, seq, prm)
    return x2d.reshape(batch, seq, d)
```

```python
import functools

import jax
import jax.numpy as jnp
from jax import lax
from jax.experimental import pallas as pl
from jax.experimental.pallas import tpu as pltpu

F32 = jnp.float32
BF16 = jnp.bfloat16
I32 = jnp.int32

HEAD_DIM = 64
GMLP_WIDTH = 256
GMLP_GROUPS = 4
GMLP_CHUNK = 128
DSA_WIDTH = 384
DSA_PATTERNS = ((128, 1), (512, 4), (2048, 16))
GDN_WIDTH = 384
GDN_CONV = 5
ROT_DIM = 16
ROPE_THETA = 500000.0
N_EXPERTS = 16
EC_CAPACITY = 2
NORM_EPS = 1e-6
MASK_VALUE = -1e30

LANES = 128
SUBLANES = 8
VMEM_LIMIT_BYTES = 56 * 1024 * 1024

PAIR = 2 * HEAD_DIM
GDN_CHUNK = 64
GDN_GROUP = 4
ATT_HALO = 64
ATT_SUB = 128
OFFS_W = 40
MOE_TBLK = 256
MOE_STATIC = 6


def _cparams(*sem):
    return pltpu.CompilerParams(dimension_semantics=sem, vmem_limit_bytes=VMEM_LIMIT_BYTES)


def _dot(a, b):
    return jnp.dot(a, b, preferred_element_type=F32)


def _dot_nt(a, b):
    return lax.dot_general(a, b, (((1,), (1,)), ((), ())), preferred_element_type=F32)


def _dot_tn(a, b):
    return lax.dot_general(a, b, (((0,), (0,)), ((), ())), preferred_element_type=F32)


def _split2(x):
    hi = x.astype(BF16)
    lo = (x - hi.astype(F32)).astype(BF16)
    return hi, lo


def _split3(x):
    hi = x.astype(BF16)
    r = x - hi.astype(F32)
    mid = r.astype(BF16)
    lo = (r - mid.astype(F32)).astype(BF16)
    return hi, mid, lo


def _group_mean(x, avg):
    hi, lo = _split2(x)
    return _dot(hi, avg) + _dot(lo, avg)


def _rms(x):
    return x * lax.rsqrt(jnp.mean(x * x, axis=-1, keepdims=True) + NORM_EPS)


def _floor_to(x, pow2):
    s = pow2.bit_length() - 1
    return lax.shift_left(lax.shift_right_logical(x, s), s)


def _block_avg_matrix(width, group):
    i = jnp.arange(width)
    return jnp.where((i[:, None] // group) == (i[None, :] // group), 1.0 / group, 0.0).astype(BF16)


def _inproj_kernel(x_ref, g_ref, w_ref, *o_refs):
    x = x_ref[...]
    xn = (_rms(x) * g_ref[...]).astype(BF16)
    col = 0
    for o_ref in o_refs:
        n = o_ref.shape[-1]
        o_ref[...] = _dot(xn, w_ref[:, col:col + n])
        col += n


def _inproj(x2d, g, w, widths, tm=512):
    m, d = x2d.shape
    n = w.shape[1]
    return pl.pallas_call(
        _inproj_kernel,
        grid=(m // tm,),
        in_specs=[pl.BlockSpec((tm, d), lambda i: (i, 0)),
                  pl.BlockSpec((1, d), lambda i: (0, 0)),
                  pl.BlockSpec((d, n), lambda i: (0, 0))],
        out_specs=[pl.BlockSpec((tm, wd), lambda i: (i, 0)) for wd in widths],
        out_shape=[jax.ShapeDtypeStruct((m, wd), F32) for wd in widths],
        compiler_params=_cparams("parallel"),
        name="inproj",
    )(x2d, g, w)


def _gmlp_kernel(u_ref, v_ref, lng_ref, lnb_ref, ws_ref, bias_ref, avg_ref, o_ref):
    tm = u_ref.shape[0]
    u = jax.nn.gelu(u_ref[...])
    vf = jax.nn.gelu(v_ref[...])
    avg = avg_ref[...]
    mu = _group_mean(vf, avg)
    dv = vf - mu
    var = _group_mean(dv * dv, avg)
    vn = dv * lax.rsqrt(var + NORM_EPS) * lng_ref[...] + lnb_ref[...]
    grp = lax.broadcasted_iota(I32, (GMLP_CHUNK, GMLP_WIDTH), 1) // HEAD_DIM
    for c in range(tm // GMLP_CHUNK):
        rows = slice(c * GMLP_CHUNK, (c + 1) * GMLP_CHUNK)
        vc = vn[rows].astype(BF16)
        mixed = bias_ref[...]
        for g in range(GMLP_GROUPS):
            mixed = mixed + jnp.where(grp == g, _dot(ws_ref[g], vc), 0.0)
        o_ref[rows, :] = (u[rows] * mixed).astype(BF16)


def _gmlp(pa, ln_g, ln_b, w_s, b_s, tm=512):
    m = pa.shape[0]
    w = GMLP_WIDTH
    bias2d = jnp.repeat(b_s.T, HEAD_DIM, axis=1)
    const = lambda i: (0, 0)
    return pl.pallas_call(
        _gmlp_kernel,
        grid=(m // tm,),
        in_specs=[pl.BlockSpec((tm, w), lambda i: (i, 0)),
                  pl.BlockSpec((tm, w), lambda i: (i, 1)),
                  pl.BlockSpec((1, w), const),
                  pl.BlockSpec((1, w), const),
                  pl.BlockSpec((GMLP_GROUPS, GMLP_CHUNK, GMLP_CHUNK), lambda i: (0, 0, 0)),
                  pl.BlockSpec((GMLP_CHUNK, w), const),
                  pl.BlockSpec((w, w), const)],
        out_specs=pl.BlockSpec((tm, w), lambda i: (i, 0)),
        out_shape=jax.ShapeDtypeStruct((m, w), BF16),
        compiler_params=_cparams("parallel"),
        name="gmlp",
    )(pa, pa, ln_g.reshape(1, w), ln_b.reshape(1, w), w_s.astype(BF16), bias2d,
      _block_avg_matrix(w, HEAD_DIM))


def _rope_table_kernel(pos_ref, invf_ref, sa_ref, sb_ref, cos_ref, sina_ref, sinb_ref):
    ang = pos_ref[...].astype(F32) * invf_ref[...]
    s = jnp.sin(ang)
    cos_ref[...] = jnp.cos(ang)
    sina_ref[...] = s * sa_ref[...]
    sinb_ref[...] = s * sb_ref[...]


def _rope_tables(positions, ts=1024):
    m = positions.size
    half = ROT_DIM // 2
    inv_freq = ROPE_THETA ** (-jnp.arange(half, dtype=F32) * 2.0 / ROT_DIM)
    lane = jnp.arange(PAIR) % HEAD_DIM
    invf = jnp.where(lane < ROT_DIM, inv_freq[lane % half], 0.0).reshape(1, PAIR)
    sa = jnp.where(lane < half, -1.0, 0.0).astype(F32).reshape(1, PAIR)
    sb = jnp.where((lane >= half) & (lane < ROT_DIM), 1.0, 0.0).astype(F32).reshape(1, PAIR)
    const = lambda i: (0, 0)
    row = pl.BlockSpec((ts, PAIR), lambda i: (i, 0))
    return pl.pallas_call(
        _rope_table_kernel,
        grid=(m // ts,),
        in_specs=[pl.BlockSpec((ts, 1), lambda i: (i, 0)),
                  pl.BlockSpec((1, PAIR), const), pl.BlockSpec((1, PAIR), const),
                  pl.BlockSpec((1, PAIR), const)],
        out_specs=[row, row, row],
        out_shape=[jax.ShapeDtypeStruct((m, PAIR), F32)] * 3,
        compiler_params=_cparams("parallel"),
        name="rope_tables",
    )(positions.reshape(m, 1), invf, sa, sb)


def _dsa_prep_kernel(q_ref, k_ref, v_ref, cos_ref, sina_ref, sinb_ref, gq_ref, gk_ref, avg_ref, *refs):
    o_refs, stage = refs[:-1], refs[-1]
    tm = q_ref.shape[0]
    w = DSA_WIDTH
    cos, sina, sinb = cos_ref[...], sina_ref[...], sinb_ref[...]
    avg = avg_ref[...]
    half = ROT_DIM // 2

    def norm_rot(t, g, scale, col0):
        t = t * lax.rsqrt(_group_mean(t * t, avg) + NORM_EPS) * g
        for hp in range(w // PAIR):
            tp = t[:, hp * PAIR:(hp + 1) * PAIR]
            rot = tp * cos + pltpu.roll(tp, PAIR - half, 1) * sina + pltpu.roll(tp, half, 1) * sinb
            stage[col0 + hp] = rot * scale

    npairs = w // PAIR
    norm_rot(q_ref[...], gq_ref[...], HEAD_DIM ** -0.5, 0)
    norm_rot(k_ref[...], gk_ref[...], 1.0, npairs)
    for hp in range(npairs):
        stage[2 * npairs + hp] = v_ref[:, hp * PAIR:(hp + 1) * PAIR]
    for o_ref, (_, dil) in zip(o_refs, DSA_PATTERNS):
        for r in range(dil):
            rows = pl.ds(r, tm // dil, stride=dil) if dil > 1 else slice(None)
            for g in range(3 * npairs):
                o_ref[r, :, g * PAIR:(g + 1) * PAIR] = stage[g, rows, :].astype(BF16)


def _dsa_prep(pb, tables, batch, seq, q_norm_g, k_norm_g, tm=512):
    w = DSA_WIDTH
    const = lambda b, i: (0, 0)
    pb3 = pb.reshape(batch, seq, 3 * w)
    tab = pl.BlockSpec((None, tm, PAIR), lambda b, i: (b, i, 0))
    gq = jnp.tile(q_norm_g, w // HEAD_DIM).reshape(1, w)
    gk = jnp.tile(k_norm_g, w // HEAD_DIM).reshape(1, w)
    dils = [dil for _, dil in DSA_PATTERNS]
    return pl.pallas_call(
        _dsa_prep_kernel,
        grid=(batch, seq // tm),
        in_specs=[pl.BlockSpec((None, tm, w), lambda b, i: (b, i, 0)),
                  pl.BlockSpec((None, tm, w), lambda b, i: (b, i, 1)),
                  pl.BlockSpec((None, tm, w), lambda b, i: (b, i, 2)),
                  tab, tab, tab,
                  pl.BlockSpec((1, w), const), pl.BlockSpec((1, w), const),
                  pl.BlockSpec((w, w), const)],
        out_specs=[pl.BlockSpec((None, dil, tm // dil, 3 * w), lambda b, i: (b, 0, i, 0)) for dil in dils],
        out_shape=[jax.ShapeDtypeStruct((batch, dil, seq // dil, 3 * w), BF16) for dil in dils],
        scratch_shapes=[pltpu.VMEM((3 * w // PAIR, tm, PAIR), F32)],
        compiler_params=_cparams("parallel", "parallel"),
        name="dsa_prep",
    )(pb3, pb3, pb3, *[t.reshape(batch, seq, PAIR) for t in tables], gq, gk, _block_avg_matrix(w, HEAD_DIM))


def _dsa_kernel(q_ref, kp_ref, k_ref, kn_ref, vp_ref, v_ref, vn_ref, o_ref, lse_ref, kbuf, vtbuf, *,
                steps, sub_len):
    t = q_ref.shape[0]
    i = pl.program_id(2)
    kbuf[0:ATT_HALO, :] = kp_ref[...]
    kbuf[ATT_HALO:ATT_HALO + t, :] = k_ref[...]
    kbuf[ATT_HALO + t:, :] = kn_ref[...]
    npairs = DSA_WIDTH // PAIR
    vfull = jnp.concatenate([vp_ref[...], v_ref[...], vn_ref[...]], axis=0).astype(F32)
    for hp in range(npairs):
        vtbuf[hp] = vfull[:, hp * PAIR:(hp + 1) * PAIR].T.astype(BF16)

    nk = ATT_SUB + 2 * ATT_HALO
    lane = lax.broadcasted_iota(I32, (1, PAIR), 1)
    lane_mask = (lane < HEAD_DIM, lane >= HEAD_DIM)
    row = lax.broadcasted_iota(I32, (PAIR, 1), 0)
    row_mask = (row < HEAD_DIM, row >= HEAD_DIM)
    stat_row = lax.broadcasted_iota(I32, (LANES, ATT_SUB), 0)
    for s in range(t // ATT_SUB):
        rows = slice(s * ATT_SUB, (s + 1) * ATT_SUB)
        base = i * t + s * ATT_SUB
        kj = base - ATT_HALO + lax.broadcasted_iota(I32, (nk, 1), 0)
        qi = base + lax.broadcasted_iota(I32, (1, ATT_SUB), 1)
        valid = (jnp.abs(kj - qi) <= steps) & (kj >= 0) & (kj < sub_len)
        kk = kbuf[s * ATT_SUB:s * ATT_SUB + nk, :]
        stats = jnp.zeros((LANES, ATT_SUB), F32)
        for hp in range(npairs):
            cols = slice(hp * PAIR, (hp + 1) * PAIR)
            qp = q_ref[rows, cols]
            kp = kk[:, cols]
            vt = vtbuf[hp, :, s * ATT_SUB:s * ATT_SUB + nk]
            acc_t = jnp.zeros((PAIR, ATT_SUB), F32)
            for hh in range(2):
                km = jnp.where(lane_mask[hh], kp, jnp.zeros_like(kp))
                sc = jnp.where(valid, _dot_nt(km, qp), MASK_VALUE)
                m = jnp.max(sc, axis=0, keepdims=True)
                p = jnp.exp(sc - m)
                l = jnp.sum(p, axis=0, keepdims=True)
                vth = jnp.where(row_mask[hh], vt, jnp.zeros_like(vt))
                acc_t = acc_t + _dot(vth, p.astype(BF16)) * (1.0 / l)
                stats = jnp.where(stat_row == 2 * hp + hh, m + jnp.log(l), stats)
            o_ref[rows, cols] = acc_t.T.astype(BF16)
        lse_ref[rows, :] = stats.T


def _dsa_pattern(qkv, window, dil):
    batch, _, sub_len, _ = qkv.shape
    w = DSA_WIDTH
    steps = window // (2 * dil)
    assert steps <= ATT_HALO
    t = min(512, sub_len)
    nb64 = sub_len // ATT_HALO
    r64 = t // ATT_HALO

    def main(which):
        return pl.BlockSpec((None, None, t, w), lambda b, r, i: (b, r, i, which))

    def prev(which):
        return pl.BlockSpec((None, None, ATT_HALO, w),
                            lambda b, r, i: (b, r, jnp.maximum(i * r64 - 1, 0), which))

    def nxt(which):
        return pl.BlockSpec((None, None, ATT_HALO, w),
                            lambda b, r, i: (b, r, jnp.minimum((i + 1) * r64, nb64 - 1), which))

    out = lambda wd: pl.BlockSpec((None, None, t, wd), lambda b, r, i: (b, r, i, 0))
    return pl.pallas_call(
        functools.partial(_dsa_kernel, steps=steps, sub_len=sub_len),
        grid=(batch, dil, sub_len // t),
        in_specs=[main(0), prev(1), main(1), nxt(1), prev(2), main(2), nxt(2)],
        out_specs=[out(w), out(LANES)],
        out_shape=[jax.ShapeDtypeStruct((batch, dil, sub_len, w), BF16),
                   jax.ShapeDtypeStruct((batch, dil, sub_len, LANES), F32)],
        scratch_shapes=[pltpu.VMEM((t + 2 * ATT_HALO, w), BF16),
                        pltpu.VMEM((w // PAIR, PAIR, t + 2 * ATT_HALO), BF16)],
        compiler_params=_cparams("parallel", "parallel", "parallel"),
        name=f"dsa_d{dil}",
    )(*([qkv] * 7))


def _dsa_combine_kernel(*refs):
    n = len(DSA_PATTERNS)
    o_refs, lse_refs, expand_ref, y_ref = refs[:n], refs[n:2 * n], refs[2 * n], refs[2 * n + 1]
    stages = iter(refs[2 * n + 2:])
    tm = y_ref.shape[0]
    npairs = DSA_WIDTH // PAIR
    outs, lses = [], []
    for o_ref, lse_ref, (_, dil) in zip(o_refs, lse_refs, DSA_PATTERNS):
        if dil == 1:
            outs.append([o_ref[0, :, hp * PAIR:(hp + 1) * PAIR].astype(F32) for hp in range(npairs)])
            lses.append(lse_ref[0])
        else:
            so, sl = next(stages), next(stages)
            for r in range(dil):
                rows = pl.ds(r, tm // dil, stride=dil)
                for hp in range(npairs):
                    so[hp, rows, :] = o_ref[r, :, hp * PAIR:(hp + 1) * PAIR].astype(F32)
                sl[rows, :] = lse_ref[r]
            outs.append([so[hp] for hp in range(npairs)])
            lses.append(sl[...])
    top = functools.reduce(jnp.maximum, lses)
    es = [jnp.exp(lse - top) for lse in lses]
    inv = 1.0 / sum(es)
    weights = [_group_mean(e * inv, expand_ref[...]) for e in es]
    for hp in range(npairs):
        cols = slice(hp * PAIR, (hp + 1) * PAIR)
        y_ref[:, cols] = sum(o[hp] * wgt[:, cols] for o, wgt in zip(outs, weights)).astype(BF16)


def _dsa_combine(outs, lses, batch, seq, tm=512):
    w = DSA_WIDTH
    dils = [dil for _, dil in DSA_PATTERNS]
    lane = jnp.arange(w) // HEAD_DIM
    expand = (jnp.arange(LANES)[:, None] == lane[None, :]).astype(BF16)
    spec = lambda dil, wd: pl.BlockSpec((None, dil, tm // dil, wd), lambda b, i: (b, 0, i, 0))
    scratch = []
    for dil in dils:
        if dil > 1:
            scratch += [pltpu.VMEM((w // PAIR, tm, PAIR), F32), pltpu.VMEM((tm, LANES), F32)]
    return pl.pallas_call(
        _dsa_combine_kernel,
        grid=(batch, seq // tm),
        in_specs=[spec(dil, w) for dil in dils] + [spec(dil, LANES) for dil in dils]
                 + [pl.BlockSpec((LANES, w), lambda b, i: (0, 0))],
        out_specs=pl.BlockSpec((None, tm, w), lambda b, i: (b, i, 0)),
        out_shape=jax.ShapeDtypeStruct((batch, seq, w), BF16),
        scratch_shapes=scratch,
        compiler_params=_cparams("parallel", "parallel"),
        name="dsa_combine",
    )(*outs, *lses, expand).reshape(batch * seq, w)


def _dsa(qkvs, batch, seq):
    res = [_dsa_pattern(qkv, window, dil) for qkv, (window, dil) in zip(qkvs, DSA_PATTERNS)]
    return _dsa_combine([r[0] for r in res], [r[1] for r in res], batch, seq)


def _gdn_kernel(xq_ref, xk_ref, xv_ref, gate_ref, ab_ref, cw_ref, alog_ref, dtb_ref, on_ref,
                ones_ref, avg_ref, o_ref, m_s, n_s, qp_s, op_s, el_s):
    seq = xq_ref.shape[0]
    c = GDN_CHUNK
    nch = seq // c
    halo = SUBLANES
    lane = lax.broadcasted_iota(I32, (1, PAIR), 1)
    head_mask = (lane < HEAD_DIM, lane >= HEAD_DIM)
    ti = lax.broadcasted_iota(I32, (c, c), 0)
    si = lax.broadcasted_iota(I32, (c, c), 1)
    incl = (si <= ti, si >= ti)
    strict = (si < ti, si > ti)
    tri_incl = tuple(jnp.where(m, 1.0, 0.0).astype(BF16) for m in incl)
    bd_ones = ones_ref[...]
    lane8 = lax.broadcasted_iota(I32, (1, LANES), 1)
    neg_a = -jnp.exp(alog_ref[...])
    dtb = dtb_ref[...]

    def conv_silu(x_ref, which, r0, ci):
        main = x_ref[pl.ds(r0, c), :]
        before = x_ref[pl.ds(jnp.maximum(r0 - halo, 0), halo), :]
        after = x_ref[pl.ds(jnp.minimum(r0 + c, seq - halo), halo), :]
        before = jnp.where(ci > 0, before, 0.0)
        after = jnp.where(ci < nch - 1, after, 0.0)
        win = jnp.concatenate([before, main, after], axis=0)
        n = c + 2 * halo
        y = jnp.zeros((c, PAIR), F32)
        for j in range(GDN_CONV):
            shifted = win if j == GDN_CONV // 2 else pltpu.roll(win, (GDN_CONV // 2 - j) % n, 0)
            y = y + shifted[halo:halo + c] * cw_ref[which, j:j + 1, :]
        return y * jax.nn.sigmoid(y)

    def l2n(x):
        hi, lo = _split2(x * x)
        ss = _dot(hi, bd_ones) + _dot(lo, bd_ones)
        return x * lax.rsqrt(ss + NORM_EPS)

    eye = jnp.where(si == ti, 1.0, 0.0)
    li = lax.broadcasted_iota(I32, (PAIR, PAIR), 0) // HEAD_DIM
    lj = lax.broadcasted_iota(I32, (PAIR, PAIR), 1) // HEAD_DIM
    same_head = li == lj

    def transform(it, carry):
        dirs = []
        for gi in range(GDN_GROUP):
            ci = it * GDN_GROUP + gi
            r0 = pl.multiple_of(ci * c, c)
            q = l2n(conv_silu(xq_ref, 0, r0, ci)) * (HEAD_DIM ** -0.5)
            k = l2n(conv_silu(xk_ref, 1, r0, ci))
            v = conv_silu(xv_ref, 2, r0, ci)
            ab = ab_ref[pl.ds(r0, c), :]
            gb = jnp.where(lane8 < 4, neg_a * jax.nn.softplus(ab + dtb), jax.nn.sigmoid(ab))
            g3 = _split3(gb)
            for d in range(2):
                dirs.append(dict(ci=ci, r0=r0, d=d, q=q, k=k, v=v, gb=gb, g3=g3))
        for dd in dirs:
            dd["gc"] = sum(_dot(tri_incl[dd["d"]], part) for part in dd["g3"])
        chains = []
        for dd in dirs:
            d, gc, gb, k = dd["d"], dd["gc"], dd["gb"], dd["k"]
            gct = gc.T
            g_last = gc[c - 1:c, :] if d == 0 else gc[0:1, :]
            dd["chains"] = []
            for hh in range(2):
                col = 2 * d + hh
                gc_c = gc[:, col:col + 1]
                beta = gb[:, 4 + col:5 + col]
                km = jnp.where(head_mask[hh], k, 0.0)
                kb = km * beta
                eg = jnp.exp(gc_c)
                gl = g_last[:, col:col + 1]
                ch = dict(d=d,
                          decay=jnp.where(incl[d], jnp.exp(jnp.minimum(gc_c - gct[col:col + 1, :], 0.0)), 0.0),
                          kmb=km.astype(BF16), kbb=kb.astype(BF16),
                          qmb=jnp.where(head_mask[hh], dd["q"], 0.0).astype(BF16),
                          x0=jnp.concatenate([jnp.where(head_mask[hh], dd["v"], 0.0) * beta, kb * eg],
                                             axis=1).astype(BF16),
                          eg=eg, erem=jnp.exp(gl - gc_c), elast=jnp.exp(gl))
                chains.append(ch)
                dd["chains"].append(ch)
        for ch in chains:
            ch["a"] = _dot_nt(ch["kbb"], ch["kmb"])
        for ch in chains:
            ch["qkr"] = _dot_nt(ch["qmb"], ch["kmb"])
        for ch in chains:
            lmat = jnp.where(strict[ch["d"]], ch["a"] * ch["decay"], 0.0)
            ch["pw"] = lmat.astype(BF16)
            ch["t"] = eye - lmat
            ch["qk"] = jnp.where(incl[ch["d"]], ch["qkr"] * ch["decay"], 0.0).astype(BF16)
        for ch in chains:
            ch["sq"] = _dot(ch["pw"], ch["pw"])
        for ch in chains:
            ch["pw"] = ch["sq"].astype(BF16)
        for _ in range(4):
            for ch in chains:
                ch["tp"] = _dot(ch["t"].astype(BF16), ch["pw"])
            for ch in chains:
                ch["sq"] = _dot(ch["pw"], ch["pw"])
            for ch in chains:
                ch["t"] = ch["t"] + ch["tp"]
                ch["pw"] = ch["sq"].astype(BF16)
        for ch in chains:
            ch["tp"] = _dot(ch["t"].astype(BF16), ch["pw"])
        for ch in chains:
            ch["xu"] = _dot((ch["t"] + ch["tp"]).astype(BF16), ch["x0"])
        for dd in dirs:
            d, r0, ci = dd["d"], dd["r0"], dd["ci"]
            c0, c1 = dd["chains"]
            u_h = [c0["xu"][:, :PAIR].astype(BF16), c1["xu"][:, :PAIR].astype(BF16)]
            w_h = [c0["xu"][:, PAIR:].astype(BF16), c1["xu"][:, PAIR:].astype(BF16)]
            kd = (dd["k"] * jnp.where(head_mask[0], c0["erem"], c1["erem"])).astype(BF16)
            m_s[d, ci] = jnp.where(same_head, _dot_tn(kd, w_h[0] + w_h[1]), 0.0).astype(BF16)
            n_s[d, ci] = jnp.where(same_head, _dot_tn(kd, u_h[0] + u_h[1]), 0.0)
            qg = dd["q"] * jnp.where(head_mask[0], c0["eg"], c1["eg"])
            qp_s[d, pl.ds(r0, c), :] = (qg - _dot(c0["qk"], w_h[0]) - _dot(c1["qk"], w_h[1])).astype(BF16)
            op_s[d, pl.ds(r0, c), :] = _dot(c0["qk"], u_h[0]) + _dot(c1["qk"], u_h[1])
            el_s[d, pl.ds(ci, 1), :] = jnp.where(head_mask[0], c0["elast"], c1["elast"])
        return carry

    lax.fori_loop(0, nch // GDN_GROUP, transform, 0)

    def scan(i, states):
        cis = (i, nch - 1 - i)
        rows = [pl.ds(pl.multiple_of(ci * c, c), c) for ci in cis]
        stb = [st.astype(BF16) for st in states]
        trans = [_dot(m_s[d, cis[d]], stb[d]) for d in range(2)]
        outs = [_dot(qp_s[d, rows[d], :], stb[d]) for d in range(2)]
        new_states = []
        for d in range(2):
            op_s[d, rows[d], :] += outs[d]
            new_states.append(states[d] * el_s[d, pl.ds(cis[d], 1), :] - trans[d] + n_s[d, cis[d]])
        return tuple(new_states)

    zero = jnp.zeros((PAIR, PAIR), F32)
    lax.fori_loop(0, nch, scan, (zero, zero))

    avg = avg_ref[...]
    tile = 512

    def finish(ti_, carry):
        r0 = pl.multiple_of(ti_ * tile, tile)
        o = op_s[0, pl.ds(r0, tile), :] + op_s[1, pl.ds(r0, tile), :]
        o = o * lax.rsqrt(_group_mean(o * o, avg) + NORM_EPS) * on_ref[...]
        gate = gate_ref[pl.ds(r0, tile), :]
        o_ref[pl.ds(r0, tile), :] = (o * (gate * jax.nn.sigmoid(gate))).astype(BF16)
        return carry

    lax.fori_loop(0, seq // tile, finish, 0)


def _gdn(pc, batch, seq, conv_w, a_log, dt_bias, o_norm_g):
    w = GDN_WIDTH
    pairs = w // PAIR
    pc3 = pc.reshape(batch, seq, pc.shape[-1])
    cw = conv_w.reshape(GDN_CONV, 3, w).transpose(1, 0, 2)
    cw = jnp.pad(cw, ((0, 0), (0, SUBLANES - GDN_CONV), (0, 0)))

    def pair_lanes(p):
        x = p.reshape(2, pairs, 2).transpose(1, 0, 2).reshape(pairs, 1, 4)
        return jnp.pad(x, ((0, 0), (0, 0), (0, LANES - 4)))

    nblk = w // PAIR

    def col(offset):
        return pl.BlockSpec((None, seq, PAIR), lambda b, hp: (b, 0, offset + hp))

    nch = seq // GDN_CHUNK
    return pl.pallas_call(
        _gdn_kernel,
        grid=(batch, pairs),
        in_specs=[col(0), col(nblk), col(2 * nblk), col(3 * nblk), col(4 * nblk),
                  pl.BlockSpec((3, SUBLANES, PAIR), lambda b, hp: (0, 0, hp)),
                  pl.BlockSpec((None, 1, LANES), lambda b, hp: (hp, 0, 0)),
                  pl.BlockSpec((None, 1, LANES), lambda b, hp: (hp, 0, 0)),
                  pl.BlockSpec((1, PAIR), lambda b, hp: (0, 0)),
                  pl.BlockSpec((PAIR, PAIR), lambda b, hp: (0, 0)),
                  pl.BlockSpec((PAIR, PAIR), lambda b, hp: (0, 0))],
        out_specs=pl.BlockSpec((None, seq, PAIR), lambda b, hp: (b, 0, hp)),
        out_shape=jax.ShapeDtypeStruct((batch, seq, w), BF16),
        scratch_shapes=[pltpu.VMEM((2, nch, PAIR, PAIR), BF16),
                        pltpu.VMEM((2, nch, PAIR, PAIR), F32),
                        pltpu.VMEM((2, seq, PAIR), BF16),
                        pltpu.VMEM((2, seq, PAIR), F32),
                        pltpu.VMEM((2, nch, PAIR), F32)],
        compiler_params=_cparams("parallel", "parallel"),
        name="gdn",
    )(pc3, pc3, pc3, pc3, pc3, cw, pair_lanes(a_log), pair_lanes(dt_bias),
      jnp.tile(o_norm_g, 2).reshape(1, PAIR),
      (_block_avg_matrix(PAIR, HEAD_DIM) * HEAD_DIM).astype(BF16),
      _block_avg_matrix(PAIR, HEAD_DIM)).reshape(batch * seq, w)


def _outproj_kernel(x_ref, ya_ref, yb_ref, yc_ref, wa_ref, wb_ref, wc_ref, g_ref, x1_ref, h_ref):
    x1 = (x_ref[...] + _dot(ya_ref[...], wa_ref[...]) + _dot(yb_ref[...], wb_ref[...])
          + _dot(yc_ref[...], wc_ref[...]))
    x1_ref[...] = x1
    h_ref[...] = (_rms(x1) * g_ref[...]).astype(BF16)


def _outproj(x2d, ya, yb, yc, w_out, g_ffn, tm=512):
    m, d = x2d.shape
    wa = w_out[:GMLP_WIDTH]
    wb = w_out[GMLP_WIDTH:GMLP_WIDTH + DSA_WIDTH]
    wc = w_out[GMLP_WIDTH + DSA_WIDTH:]
    const = lambda i: (0, 0)
    row = lambda wd: pl.BlockSpec((tm, wd), lambda i: (i, 0))
    return pl.pallas_call(
        _outproj_kernel,
        grid=(m // tm,),
        in_specs=[row(d), row(GMLP_WIDTH), row(DSA_WIDTH), row(GDN_WIDTH),
                  pl.BlockSpec(wa.shape, const), pl.BlockSpec(wb.shape, const),
                  pl.BlockSpec(wc.shape, const), pl.BlockSpec((1, d), const)],
        out_specs=[row(d), row(d)],
        out_shape=[jax.ShapeDtypeStruct((m, d), F32), jax.ShapeDtypeStruct((m, d), BF16)],
        compiler_params=_cparams("parallel"),
        name="outproj",
    )(x2d, ya, yb, yc, wa, wb, wc, g_ffn.reshape(1, d))


def _router_kernel(h_ref, wr_ref, tri_ref, pos_ref, aff_ref, offs_ref, bits_s, *, cap):
    seq = h_ref.shape[0]
    nblk = seq // LANES
    logits = _dot_nt(wr_ref[...], h_ref[...])
    ex = jnp.exp(logits - jnp.max(logits, axis=0, keepdims=True))
    aff = ex / jnp.sum(ex, axis=0, keepdims=True)
    aff_ref[...] = aff
    bits = lax.bitcast_convert_type(aff, I32)
    bits_s[...] = bits

    def bisect(it, prefix):
        cand = prefix | jnp.left_shift(jnp.int32(1), 30 - it)
        cnt = jnp.sum(jnp.where(bits_s[...] >= cand, 1.0, 0.0), axis=1, keepdims=True)
        return jnp.where(cnt >= cap, cand, prefix)

    thr = lax.fori_loop(0, 31, bisect, jnp.zeros((N_EXPERTS, 1), I32))
    n_gt = jnp.sum(jnp.where(bits > thr, 1.0, 0.0), axis=1, keepdims=True)
    need = cap - n_gt
    tri = tri_ref[...]
    lane = lax.broadcasted_iota(I32, (N_EXPERTS, LANES), 1)

    off_eq = jnp.zeros((N_EXPERTS, 1), F32)
    off_sel = jnp.zeros((N_EXPERTS, 1), F32)
    offs = jnp.zeros((N_EXPERTS, LANES), F32)
    for j in range(nblk):
        cols = slice(j * LANES, (j + 1) * LANES)
        bj = bits_s[:, cols]
        eq = bj == thr
        eq_f = jnp.where(eq, 1.0, 0.0)
        rank = off_eq + _dot(eq_f.astype(BF16), tri)
        sel = (bj > thr) | (eq & (rank < need))
        sel_f = jnp.where(sel, 1.0, 0.0)
        pos = off_sel + _dot(sel_f.astype(BF16), tri)
        pos_ref[:, cols] = jnp.where(sel, pos, -1.0)
        offs = jnp.where(lane == j, off_sel, offs)
        off_eq = off_eq + jnp.sum(eq_f, axis=1, keepdims=True)
        off_sel = off_sel + jnp.sum(sel_f, axis=1, keepdims=True)
    offs_ref[...] = jnp.where(lane == nblk, off_sel, offs).astype(I32)


def _router(h3, w_router):
    batch, seq, d = h3.shape
    cap = EC_CAPACITY * seq // N_EXPERTS
    i = jnp.arange(LANES)
    tri = jnp.where(i[:, None] < i[None, :], 1.0, 0.0).astype(BF16)
    out3 = lambda wd: pl.BlockSpec((None, N_EXPERTS, wd), lambda b: (b, 0, 0))
    return pl.pallas_call(
        functools.partial(_router_kernel, cap=cap),
        grid=(batch,),
        in_specs=[pl.BlockSpec((None, seq, d), lambda b: (b, 0, 0)),
                  pl.BlockSpec((N_EXPERTS, d), lambda b: (0, 0)),
                  pl.BlockSpec((LANES, LANES), lambda b: (0, 0))],
        out_specs=[out3(seq), out3(seq), out3(LANES)],
        out_shape=[jax.ShapeDtypeStruct((batch, N_EXPERTS, seq), F32),
                   jax.ShapeDtypeStruct((batch, N_EXPERTS, seq), F32),
                   jax.ShapeDtypeStruct((batch, N_EXPERTS, LANES), I32)],
        scratch_shapes=[pltpu.VMEM((N_EXPERTS, seq), I32)],
        compiler_params=_cparams("parallel"),
        name="router",
    )(h3, w_router.T.astype(BF16), tri)


def _moe_ffn_kernel(offs_ref, h_ref, pos_ref, aff_ref, wg_ref, wu_ref, wd_ref, ye_ref, x_s, gate_s, w_s, *, cap):
    nblk, tblk = pos_ref.shape
    per = tblk // LANES
    e = pl.program_id(0)
    b = pl.program_id(1)
    base = (b * N_EXPERTS + e) * OFFS_W

    @pl.when(b == 0)
    def _():
        w_s[0] = wg_ref[...].astype(BF16)
        w_s[1] = wu_ref[...].astype(BF16)
        w_s[2] = wd_ref[...].astype(BF16)

    rid = lax.broadcasted_iota(I32, (LANES, tblk), 0)

    def contribution(j, want):
        hit = pos_ref[pl.ds(j, 1), :] == want
        hj = h_ref[pl.ds(pl.multiple_of(j * tblk, tblk), tblk), :]
        return (_dot(jnp.where(hit, 1.0, 0.0).astype(BF16), hj),
                jnp.sum(jnp.where(hit, aff_ref[pl.ds(j, 1), :], 0.0), axis=1, keepdims=True))

    for c in range(cap // LANES):
        rows = slice(c * LANES, (c + 1) * LANES)
        first = jnp.int32(0)
        for j in range(nblk):
            first = first + (offs_ref[base + (j + 1) * per] <= c * LANES).astype(I32)
        x = jnp.zeros((LANES, h_ref.shape[1]), F32)
        gate = jnp.zeros((LANES, 1), F32)
        for k in range(MOE_STATIC):
            j = first + k
            shift = jnp.where(j < nblk, c * LANES, -cap)
            dx, dg = contribution(jnp.minimum(j, nblk - 1), (rid + shift).astype(F32))
            x, gate = x + dx, gate + dg
        x_s[rows, :] = x
        gate_s[rows, :] = jnp.broadcast_to(gate, (LANES, LANES))
        want = (rid + c * LANES).astype(F32)

        def rest(j, carry):
            @pl.when(offs_ref[base + j * per] < (c + 1) * LANES)
            def _():
                dx, dg = contribution(j, want)
                x_s[rows, :] += dx
                gate_s[rows, :] += dg

            return carry

        lax.fori_loop(first + MOE_STATIC, nblk, rest, 0)

    half = cap // 2
    for r in range(2):
        rows = slice(r * half, (r + 1) * half)
        x = x_s[rows, :].astype(BF16)
        g = _dot(x, w_s[0])
        u = _dot(x, w_s[1])
        hid = (g * jax.nn.sigmoid(g) * u).astype(BF16)
        ye_ref[rows, :] = (_dot(hid, w_s[2]) * gate_s[rows, 0:1]).astype(BF16)


def _moe_ffn(offs_flat, h3, pos4, aff4, wg, wu, wd, layer):
    batch, seq, d = h3.shape
    cap = EC_CAPACITY * seq // N_EXPERTS
    nblk, tblk = pos4.shape[2:]
    row = pl.BlockSpec((None, None, nblk, tblk), lambda e, b, o: (b, e, 0, 0))
    wspec = pl.BlockSpec((None, None, d, d), lambda e, b, o: (layer, e, 0, 0))
    return pl.pallas_call(
        functools.partial(_moe_ffn_kernel, cap=cap),
        grid_spec=pltpu.PrefetchScalarGridSpec(
            num_scalar_prefetch=1,
            grid=(N_EXPERTS, batch),
            in_specs=[pl.BlockSpec((None, seq, d), lambda e, b, o: (b, 0, 0)), row, row,
                      wspec, wspec, wspec],
            out_specs=pl.BlockSpec((None, None, cap, d), lambda e, b, o: (b, e, 0, 0)),
            scratch_shapes=[pltpu.VMEM((cap, d), F32), pltpu.VMEM((cap, LANES), F32),
                            pltpu.VMEM((3, d, d), BF16)]),
        out_shape=jax.ShapeDtypeStruct((batch, N_EXPERTS, cap, d), BF16),
        compiler_params=_cparams("arbitrary", "arbitrary"),
        name="moe_ffn",
    )(offs_flat, h3, pos4, aff4, wg, wu, wd)


def _moe_combine_kernel(offs_ref, x1_ref, ye_ref, ptok_ref, p_ref, wple_ref, wpg_ref, gple_ref, gpg_ref,
                        o_ref, acc_s, *, cap):
    tb = x1_ref.shape[0]
    b = pl.program_id(0)
    jb = pl.program_id(1)
    nsub = tb // LANES
    pack = 2 * SUBLANES

    def offsets(sub, e):
        at = (b * N_EXPERTS + e) * OFFS_W + jb * nsub + sub
        return offs_ref[at], offs_ref[at + 1]

    def scatter(win, align, group):
        cid = lax.broadcasted_iota(I32, (LANES, win), 1)
        for sub in range(nsub):
            rows = slice(sub * LANES, (sub + 1) * LANES)
            acc = x1_ref[rows, :]
            for e0 in range(0, N_EXPERTS, group):
                hits, wins = [], []
                for e in range(e0, e0 + group):
                    lo, _ = offsets(sub, e)
                    start = pl.multiple_of(jnp.minimum(_floor_to(lo, align), cap - win), align)
                    hit = ptok_ref[rows, e:e + 1] == (cid + start).astype(F32)
                    hits.append(jnp.where(hit, 1.0, 0.0).astype(BF16))
                    wins.append(ye_ref[e, pl.ds(start, win), :])
                acc = acc + _dot(jnp.concatenate(hits, axis=1), jnp.concatenate(wins, axis=0))
            acc_s[rows, :] = acc

    narrow = jnp.bool_(True)
    for sub in range(nsub):
        for e in range(N_EXPERTS):
            lo, hi = offsets(sub, e)
            narrow = narrow & (hi - _floor_to(lo, pack) <= LANES)

    @pl.when(narrow)
    def _():
        scatter(LANES, pack, 2)

    @pl.when(jnp.logical_not(narrow))
    def _():
        scatter(2 * LANES, LANES, 1)

    x2 = acc_s[...]
    emb = _rms(_dot(p_ref[...].astype(BF16), wple_ref[...])) * gple_ref[...]
    gate = jax.nn.sigmoid(_dot((_rms(x2) * gpg_ref[...]).astype(BF16), wpg_ref[...]))
    o_ref[...] = x2 + emb * gate


def _moe_combine(offs_flat, x1_3, ye, ptok, p4, layer, w_ple, w_pg, g_ple, g_pg, tb=256):
    batch, seq, d = x1_3.shape
    cap = ye.shape[2]
    pd = p4.shape[-1]
    const = lambda b, j, o: (0, 0)
    row = lambda wd: pl.BlockSpec((None, tb, wd), lambda b, j, o: (b, j, 0))
    return pl.pallas_call(
        functools.partial(_moe_combine_kernel, cap=cap),
        grid_spec=pltpu.PrefetchScalarGridSpec(
            num_scalar_prefetch=1,
            grid=(batch, seq // tb),
            in_specs=[row(d),
                      pl.BlockSpec((None, N_EXPERTS, cap, d), lambda b, j, o: (b, 0, 0, 0)),
                      row(N_EXPERTS),
                      pl.BlockSpec((None, None, tb, pd), lambda b, j, o: (layer, b, j, 0)),
                      pl.BlockSpec((pd, d), const), pl.BlockSpec((d, d), const),
                      pl.BlockSpec((1, d), const), pl.BlockSpec((1, d), const)],
            out_specs=row(d),
            scratch_shapes=[pltpu.VMEM((tb, d), F32)]),
        out_shape=jax.ShapeDtypeStruct((batch, seq, d), F32),
        compiler_params=_cparams("arbitrary", "arbitrary"),
        name="moe_combine_ple",
    )(offs_flat, x1_3, ye, ptok, p4, w_ple, w_pg, g_ple.reshape(1, d), g_pg.reshape(1, d))


def _layer(x2d, p4, tables, batch, seq, prm):
    d = x2d.shape[-1]
    pa, pb, pc = _inproj(x2d, prm["g_mix"].reshape(1, d), prm["w_in"],
                         (2 * GMLP_WIDTH, 3 * DSA_WIDTH, prm["w_in"].shape[1] - 2 * GMLP_WIDTH - 3 * DSA_WIDTH))
    ya = _gmlp(pa, prm["ln_v_g"], prm["ln_v_b"], prm["w_s"], prm["b_s"])
    yb = _dsa(_dsa_prep(pb, tables, batch, seq, prm["q_norm_g"], prm["k_norm_g"]), batch, seq)
    yc = _gdn(pc, batch, seq, prm["conv_w"], prm["a_log"], prm["dt_bias"], prm["o_norm_g"])
    x1, h = _outproj(x2d, ya, yb, yc, prm["w_out"], prm["g_ffn"])
    h3 = h.reshape(batch, seq, d)
    pos, aff, offs = _router(h3, prm["w_router"])
    offs_flat = offs[:, :, :OFFS_W].reshape(-1)
    nblk = seq // MOE_TBLK
    ye = _moe_ffn(offs_flat, h3, pos.reshape(batch, N_EXPERTS, nblk, MOE_TBLK),
                  aff.reshape(batch, N_EXPERTS, nblk, MOE_TBLK),
                  prm["w_e_gate"], prm["w_e_up"], prm["w_e_down"], prm["layer"])
    x2 = _moe_combine(offs_flat, x1.reshape(batch, seq, d), ye, pos.transpose(0, 2, 1), p4, prm["layer"],
                      prm["w_ple"], prm["w_ple_gate"], prm["g_ple"], prm["g_ple_gate"])
    return x2.reshape(batch * seq, d)


def _arrange_in_weights(w_in):
    heads = GDN_WIDTH // HEAD_DIM
    pairs = GDN_WIDTH // PAIR
    first = w_in.shape[-1] - 4 * heads
    cols = [first + which * 2 * heads + d * heads + 2 * hp + hh
            for hp in range(pairs) for which in range(2) for d in range(2) for hh in range(2)]
    gates = w_in[:, :, jnp.array(cols)].reshape(w_in.shape[:2] + (pairs, 8))
    gates = jnp.pad(gates, ((0, 0), (0, 0), (0, 0), (0, LANES - 8))).reshape(w_in.shape[:2] + (pairs * LANES,))
    return jnp.concatenate([w_in[:, :, :first], gates], axis=-1).astype(BF16)


def kernel(x, p, positions, g_mix, w_in, ln_v_g, ln_v_b, w_s, b_s, q_norm_g, k_norm_g, conv_w, a_log, dt_bias,
           o_norm_g, w_out, g_ffn, w_router, w_e_gate, w_e_up, w_e_down, w_ple, g_ple, g_ple_gate, w_ple_gate):
    batch, seq, d = x.shape
    depth = p.shape[0]
    w_in_b = _arrange_in_weights(w_in)
    tables = _rope_tables(positions)
    x2d = x.reshape(batch * seq, d)
    for i in range(depth):
        prm = dict(g_mix=g_mix[i], w_in=w_in_b[i], ln_v_g=ln_v_g[i], ln_v_b=ln_v_b[i], w_s=w_s[i], b_s=b_s[i],
                   q_norm_g=q_norm_g[i], k_norm_g=k_norm_g[i], conv_w=conv_w[i], a_log=a_log[i],
                   dt_bias=dt_bias[i], o_norm_g=o_norm_g[i], w_out=w_out[i].astype(BF16), g_ffn=g_ffn[i],
                   w_router=w_router[i], w_e_gate=w_e_gate, w_e_up=w_e_up, w_e_down=w_e_down, layer=i,
                   w_ple=w_ple[i].astype(BF16), g_ple=g_ple[i],
                   g_ple_gate=g_ple_gate[i], w_ple_gate=w_ple_gate[i].astype(BF16))
        x2d = _layer(x2d, p, tables, batch, seq, prm)
    return x2d.reshape(batch, seq, d)
```

```python
import functools

import jax
import jax.numpy as jnp
from jax import lax
from jax.experimental import pallas as pl
from jax.experimental.pallas import tpu as pltpu

F32 = jnp.float32
BF16 = jnp.bfloat16
I32 = jnp.int32

HEAD_DIM = 64
GMLP_WIDTH = 256
GMLP_GROUPS = 4
GMLP_CHUNK = 128
DSA_WIDTH = 384
DSA_PATTERNS = ((128, 1), (512, 4), (2048, 16))
GDN_WIDTH = 384
GDN_CONV = 5
ROT_DIM = 16
ROPE_THETA = 500000.0
N_EXPERTS = 16
EC_CAPACITY = 2
NORM_EPS = 1e-6
MASK_VALUE = -1e30

LANES = 128
SUBLANES = 8
VMEM_LIMIT_BYTES = 56 * 1024 * 1024

PAIR = 2 * HEAD_DIM
GDN_CHUNK = 64
GDN_GROUP = 8
ATT_HALO = 64
ATT_SUB = 128
OFFS_W = 40
MOE_TBLK = 256
MOE_STATIC = 6


def _cparams(*sem):
    return pltpu.CompilerParams(dimension_semantics=sem, vmem_limit_bytes=VMEM_LIMIT_BYTES)


def _dot(a, b):
    return jnp.dot(a, b, preferred_element_type=F32)


def _dot_nt(a, b):
    return lax.dot_general(a, b, (((1,), (1,)), ((), ())), preferred_element_type=F32)


def _dot_tn(a, b):
    return lax.dot_general(a, b, (((0,), (0,)), ((), ())), preferred_element_type=F32)


def _split2(x):
    hi = x.astype(BF16)
    lo = (x - hi.astype(F32)).astype(BF16)
    return hi, lo


def _split3(x):
    hi = x.astype(BF16)
    r = x - hi.astype(F32)
    mid = r.astype(BF16)
    lo = (r - mid.astype(F32)).astype(BF16)
    return hi, mid, lo


def _group_mean(x, avg):
    hi, lo = _split2(x)
    return _dot(hi, avg) + _dot(lo, avg)


def _rms(x):
    return x * lax.rsqrt(jnp.mean(x * x, axis=-1, keepdims=True) + NORM_EPS)


def _floor_to(x, pow2):
    s = pow2.bit_length() - 1
    return lax.shift_left(lax.shift_right_logical(x, s), s)


def _block_avg_matrix(width, group):
    i = jnp.arange(width)
    return jnp.where((i[:, None] // group) == (i[None, :] // group), 1.0 / group, 0.0).astype(BF16)


def _inproj_kernel(x_ref, g_ref, w_ref, *o_refs):
    x = x_ref[...]
    xn = (_rms(x) * g_ref[...]).astype(BF16)
    col = 0
    for o_ref in o_refs:
        n = o_ref.shape[-1]
        o_ref[...] = _dot(xn, w_ref[:, col:col + n])
        col += n


def _inproj(x2d, g, w, widths, tm=512):
    m, d = x2d.shape
    n = w.shape[1]
    return pl.pallas_call(
        _inproj_kernel,
        grid=(m // tm,),
        in_specs=[pl.BlockSpec((tm, d), lambda i: (i, 0)),
                  pl.BlockSpec((1, d), lambda i: (0, 0)),
                  pl.BlockSpec((d, n), lambda i: (0, 0))],
        out_specs=[pl.BlockSpec((tm, wd), lambda i: (i, 0)) for wd in widths],
        out_shape=[jax.ShapeDtypeStruct((m, wd), F32) for wd in widths],
        compiler_params=_cparams("parallel"),
        name="inproj",
    )(x2d, g, w)


def _gmlp_kernel(u_ref, v_ref, lng_ref, lnb_ref, ws_ref, bias_ref, avg_ref, o_ref):
    tm = u_ref.shape[0]
    u = jax.nn.gelu(u_ref[...])
    vf = jax.nn.gelu(v_ref[...])
    avg = avg_ref[...]
    mu = _group_mean(vf, avg)
    dv = vf - mu
    var = _group_mean(dv * dv, avg)
    vn = dv * lax.rsqrt(var + NORM_EPS) * lng_ref[...] + lnb_ref[...]
    grp = lax.broadcasted_iota(I32, (GMLP_CHUNK, GMLP_WIDTH), 1) // HEAD_DIM
    for c in range(tm // GMLP_CHUNK):
        rows = slice(c * GMLP_CHUNK, (c + 1) * GMLP_CHUNK)
        vc = vn[rows].astype(BF16)
        mixed = bias_ref[...]
        for g in range(GMLP_GROUPS):
            mixed = mixed + jnp.where(grp == g, _dot(ws_ref[g], vc), 0.0)
        o_ref[rows, :] = (u[rows] * mixed).astype(BF16)


def _gmlp(pa, ln_g, ln_b, w_s, b_s, tm=512):
    m = pa.shape[0]
    w = GMLP_WIDTH
    bias2d = jnp.repeat(b_s.T, HEAD_DIM, axis=1)
    const = lambda i: (0, 0)
    return pl.pallas_call(
        _gmlp_kernel,
        grid=(m // tm,),
        in_specs=[pl.BlockSpec((tm, w), lambda i: (i, 0)),
                  pl.BlockSpec((tm, w), lambda i: (i, 1)),
                  pl.BlockSpec((1, w), const),
                  pl.BlockSpec((1, w), const),
                  pl.BlockSpec((GMLP_GROUPS, GMLP_CHUNK, GMLP_CHUNK), lambda i: (0, 0, 0)),
                  pl.BlockSpec((GMLP_CHUNK, w), const),
                  pl.BlockSpec((w, w), const)],
        out_specs=pl.BlockSpec((tm, w), lambda i: (i, 0)),
        out_shape=jax.ShapeDtypeStruct((m, w), BF16),
        compiler_params=_cparams("parallel"),
        name="gmlp",
    )(pa, pa, ln_g.reshape(1, w), ln_b.reshape(1, w), w_s.astype(BF16), bias2d,
      _block_avg_matrix(w, HEAD_DIM))


def _rope_table_kernel(pos_ref, invf_ref, sa_ref, sb_ref, cos_ref, sina_ref, sinb_ref):
    ang = pos_ref[...].astype(F32) * invf_ref[...]
    s = jnp.sin(ang)
    cos_ref[...] = jnp.cos(ang)
    sina_ref[...] = s * sa_ref[...]
    sinb_ref[...] = s * sb_ref[...]


def _rope_tables(positions, ts=1024):
    m = positions.size
    half = ROT_DIM // 2
    inv_freq = ROPE_THETA ** (-jnp.arange(half, dtype=F32) * 2.0 / ROT_DIM)
    lane = jnp.arange(PAIR) % HEAD_DIM
    invf = jnp.where(lane < ROT_DIM, inv_freq[lane % half], 0.0).reshape(1, PAIR)
    sa = jnp.where(lane < half, -1.0, 0.0).astype(F32).reshape(1, PAIR)
    sb = jnp.where((lane >= half) & (lane < ROT_DIM), 1.0, 0.0).astype(F32).reshape(1, PAIR)
    const = lambda i: (0, 0)
    row = pl.BlockSpec((ts, PAIR), lambda i: (i, 0))
    return pl.pallas_call(
        _rope_table_kernel,
        grid=(m // ts,),
        in_specs=[pl.BlockSpec((ts, 1), lambda i: (i, 0)),
                  pl.BlockSpec((1, PAIR), const), pl.BlockSpec((1, PAIR), const),
                  pl.BlockSpec((1, PAIR), const)],
        out_specs=[row, row, row],
        out_shape=[jax.ShapeDtypeStruct((m, PAIR), F32)] * 3,
        compiler_params=_cparams("parallel"),
        name="rope_tables",
    )(positions.reshape(m, 1), invf, sa, sb)


def _dsa_prep_kernel(q_ref, k_ref, v_ref, cos_ref, sina_ref, sinb_ref, gq_ref, gk_ref, avg_ref, *refs):
    o_refs, stage = refs[:-1], refs[-1]
    tm = q_ref.shape[0]
    w = DSA_WIDTH
    cos, sina, sinb = cos_ref[...], sina_ref[...], sinb_ref[...]
    avg = avg_ref[...]
    half = ROT_DIM // 2

    def norm_rot(t, g, scale, col0):
        t = t * lax.rsqrt(_group_mean(t * t, avg) + NORM_EPS) * g
        for hp in range(w // PAIR):
            tp = t[:, hp * PAIR:(hp + 1) * PAIR]
            rot = tp * cos + pltpu.roll(tp, PAIR - half, 1) * sina + pltpu.roll(tp, half, 1) * sinb
            stage[col0 + hp] = rot * scale

    npairs = w // PAIR
    norm_rot(q_ref[...], gq_ref[...], HEAD_DIM ** -0.5, 0)
    norm_rot(k_ref[...], gk_ref[...], 1.0, npairs)
    for hp in range(npairs):
        stage[2 * npairs + hp] = v_ref[:, hp * PAIR:(hp + 1) * PAIR]
    for o_ref, (_, dil) in zip(o_refs, DSA_PATTERNS):
        for r in range(dil):
            rows = pl.ds(r, tm // dil, stride=dil) if dil > 1 else slice(None)
            for g in range(3 * npairs):
                o_ref[r, :, g * PAIR:(g + 1) * PAIR] = stage[g, rows, :].astype(BF16)


def _dsa_prep(pb, tables, batch, seq, q_norm_g, k_norm_g, tm=512):
    w = DSA_WIDTH
    const = lambda b, i: (0, 0)
    pb3 = pb.reshape(batch, seq, 3 * w)
    tab = pl.BlockSpec((None, tm, PAIR), lambda b, i: (b, i, 0))
    gq = jnp.tile(q_norm_g, w // HEAD_DIM).reshape(1, w)
    gk = jnp.tile(k_norm_g, w // HEAD_DIM).reshape(1, w)
    dils = [dil for _, dil in DSA_PATTERNS]
    return pl.pallas_call(
        _dsa_prep_kernel,
        grid=(batch, seq // tm),
        in_specs=[pl.BlockSpec((None, tm, w), lambda b, i: (b, i, 0)),
                  pl.BlockSpec((None, tm, w), lambda b, i: (b, i, 1)),
                  pl.BlockSpec((None, tm, w), lambda b, i: (b, i, 2)),
                  tab, tab, tab,
                  pl.BlockSpec((1, w), const), pl.BlockSpec((1, w), const),
                  pl.BlockSpec((w, w), const)],
        out_specs=[pl.BlockSpec((None, dil, tm // dil, 3 * w), lambda b, i: (b, 0, i, 0)) for dil in dils],
        out_shape=[jax.ShapeDtypeStruct((batch, dil, seq // dil, 3 * w), BF16) for dil in dils],
        scratch_shapes=[pltpu.VMEM((3 * w // PAIR, tm, PAIR), F32)],
        compiler_params=_cparams("parallel", "parallel"),
        name="dsa_prep",
    )(pb3, pb3, pb3, *[t.reshape(batch, seq, PAIR) for t in tables], gq, gk, _block_avg_matrix(w, HEAD_DIM))


def _dsa_kernel(q_ref, kp_ref, k_ref, kn_ref, vp_ref, v_ref, vn_ref, o_ref, lse_ref, kbuf, vtbuf, *,
                steps, sub_len):
    t = q_ref.shape[0]
    i = pl.program_id(2)
    kbuf[0:ATT_HALO, :] = kp_ref[...]
    kbuf[ATT_HALO:ATT_HALO + t, :] = k_ref[...]
    kbuf[ATT_HALO + t:, :] = kn_ref[...]
    npairs = DSA_WIDTH // PAIR
    vfull = jnp.concatenate([vp_ref[...], v_ref[...], vn_ref[...]], axis=0).astype(F32)
    for hp in range(npairs):
        vtbuf[hp] = vfull[:, hp * PAIR:(hp + 1) * PAIR].T.astype(BF16)

    nk = ATT_SUB + 2 * ATT_HALO
    lane = lax.broadcasted_iota(I32, (1, PAIR), 1)
    lane_mask = (lane < HEAD_DIM, lane >= HEAD_DIM)
    row = lax.broadcasted_iota(I32, (PAIR, 1), 0)
    row_mask = (row < HEAD_DIM, row >= HEAD_DIM)
    stat_row = lax.broadcasted_iota(I32, (LANES, ATT_SUB), 0)
    for s in range(t // ATT_SUB):
        rows = slice(s * ATT_SUB, (s + 1) * ATT_SUB)
        base = i * t + s * ATT_SUB
        kj = base - ATT_HALO + lax.broadcasted_iota(I32, (nk, 1), 0)
        qi = base + lax.broadcasted_iota(I32, (1, ATT_SUB), 1)
        valid = (jnp.abs(kj - qi) <= steps) & (kj >= 0) & (kj < sub_len)
        kk = kbuf[s * ATT_SUB:s * ATT_SUB + nk, :]
        stats = jnp.zeros((LANES, ATT_SUB), F32)
        for hp in range(npairs):
            cols = slice(hp * PAIR, (hp + 1) * PAIR)
            qp = q_ref[rows, cols]
            kp = kk[:, cols]
            vt = vtbuf[hp, :, s * ATT_SUB:s * ATT_SUB + nk]
            acc_t = jnp.zeros((PAIR, ATT_SUB), F32)
            for hh in range(2):
                km = jnp.where(lane_mask[hh], kp, jnp.zeros_like(kp))
                sc = jnp.where(valid, _dot_nt(km, qp), MASK_VALUE)
                m = jnp.max(sc, axis=0, keepdims=True)
                p = jnp.exp(sc - m)
                l = jnp.sum(p, axis=0, keepdims=True)
                vth = jnp.where(row_mask[hh], vt, jnp.zeros_like(vt))
                acc_t = acc_t + _dot(vth, p.astype(BF16)) * (1.0 / l)
                stats = jnp.where(stat_row == 2 * hp + hh, m + jnp.log(l), stats)
            o_ref[rows, cols] = acc_t.T.astype(BF16)
        lse_ref[rows, :] = stats.T


def _dsa_pattern(qkv, window, dil):
    batch, _, sub_len, _ = qkv.shape
    w = DSA_WIDTH
    steps = window // (2 * dil)
    assert steps <= ATT_HALO
    t = min(512, sub_len)
    nb64 = sub_len // ATT_HALO
    r64 = t // ATT_HALO

    def main(which):
        return pl.BlockSpec((None, None, t, w), lambda b, r, i: (b, r, i, which))

    def prev(which):
        return pl.BlockSpec((None, None, ATT_HALO, w),
                            lambda b, r, i: (b, r, jnp.maximum(i * r64 - 1, 0), which))

    def nxt(which):
        return pl.BlockSpec((None, None, ATT_HALO, w),
                            lambda b, r, i: (b, r, jnp.minimum((i + 1) * r64, nb64 - 1), which))

    out = lambda wd: pl.BlockSpec((None, None, t, wd), lambda b, r, i: (b, r, i, 0))
    return pl.pallas_call(
        functools.partial(_dsa_kernel, steps=steps, sub_len=sub_len),
        grid=(batch, dil, sub_len // t),
        in_specs=[main(0), prev(1), main(1), nxt(1), prev(2), main(2), nxt(2)],
        out_specs=[out(w), out(LANES)],
        out_shape=[jax.ShapeDtypeStruct((batch, dil, sub_len, w), BF16),
                   jax.ShapeDtypeStruct((batch, dil, sub_len, LANES), F32)],
        scratch_shapes=[pltpu.VMEM((t + 2 * ATT_HALO, w), BF16),
                        pltpu.VMEM((w // PAIR, PAIR, t + 2 * ATT_HALO), BF16)],
        compiler_params=_cparams("parallel", "parallel", "parallel"),
        name=f"dsa_d{dil}",
    )(*([qkv] * 7))


def _dsa_combine_kernel(*refs):
    n = len(DSA_PATTERNS)
    o_refs, lse_refs, expand_ref, y_ref = refs[:n], refs[n:2 * n], refs[2 * n], refs[2 * n + 1]
    stages = iter(refs[2 * n + 2:])
    tm = y_ref.shape[0]
    npairs = DSA_WIDTH // PAIR
    outs, lses = [], []
    for o_ref, lse_ref, (_, dil) in zip(o_refs, lse_refs, DSA_PATTERNS):
        if dil == 1:
            outs.append([o_ref[0, :, hp * PAIR:(hp + 1) * PAIR].astype(F32) for hp in range(npairs)])
            lses.append(lse_ref[0])
        else:
            so, sl = next(stages), next(stages)
            for r in range(dil):
                rows = pl.ds(r, tm // dil, stride=dil)
                for hp in range(npairs):
                    so[hp, rows, :] = o_ref[r, :, hp * PAIR:(hp + 1) * PAIR].astype(F32)
                sl[rows, :] = lse_ref[r]
            outs.append([so[hp] for hp in range(npairs)])
            lses.append(sl[...])
    top = functools.reduce(jnp.maximum, lses)
    es = [jnp.exp(lse - top) for lse in lses]
    inv = 1.0 / sum(es)
    weights = [_group_mean(e * inv, expand_ref[...]) for e in es]
    for hp in range(npairs):
        cols = slice(hp * PAIR, (hp + 1) * PAIR)
        y_ref[:, cols] = sum(o[hp] * wgt[:, cols] for o, wgt in zip(outs, weights)).astype(BF16)


def _dsa_combine(outs, lses, batch, seq, tm=512):
    w = DSA_WIDTH
    dils = [dil for _, dil in DSA_PATTERNS]
    lane = jnp.arange(w) // HEAD_DIM
    expand = (jnp.arange(LANES)[:, None] == lane[None, :]).astype(BF16)
    spec = lambda dil, wd: pl.BlockSpec((None, dil, tm // dil, wd), lambda b, i: (b, 0, i, 0))
    scratch = []
    for dil in dils:
        if dil > 1:
            scratch += [pltpu.VMEM((w // PAIR, tm, PAIR), F32), pltpu.VMEM((tm, LANES), F32)]
    return pl.pallas_call(
        _dsa_combine_kernel,
        grid=(batch, seq // tm),
        in_specs=[spec(dil, w) for dil in dils] + [spec(dil, LANES) for dil in dils]
                 + [pl.BlockSpec((LANES, w), lambda b, i: (0, 0))],
        out_specs=pl.BlockSpec((None, tm, w), lambda b, i: (b, i, 0)),
        out_shape=jax.ShapeDtypeStruct((batch, seq, w), BF16),
        scratch_shapes=scratch,
        compiler_params=_cparams("parallel", "parallel"),
        name="dsa_combine",
    )(*outs, *lses, expand).reshape(batch * seq, w)


def _dsa(qkvs, batch, seq):
    res = [_dsa_pattern(qkv, window, dil) for qkv, (window, dil) in zip(qkvs, DSA_PATTERNS)]
    return _dsa_combine([r[0] for r in res], [r[1] for r in res], batch, seq)


def _gdn_kernel(xq_ref, xk_ref, xv_ref, gate_ref, ab_ref, cw_ref, alog_ref, dtb_ref, on_ref,
                avg_ref, o_ref, m_s, n_s, qp_s, op_s, el_s):
    seq = xq_ref.shape[0]
    c = GDN_CHUNK
    nch = seq // c
    halo = SUBLANES
    lane = lax.broadcasted_iota(I32, (1, PAIR), 1)
    head_mask = (lane < HEAD_DIM, lane >= HEAD_DIM)
    ti = lax.broadcasted_iota(I32, (c, c), 0)
    si = lax.broadcasted_iota(I32, (c, c), 1)
    incl = (si <= ti, si >= ti)
    tri_incl = tuple(jnp.where(m, 1.0, 0.0).astype(BF16) for m in incl)
    lane8 = lax.broadcasted_iota(I32, (1, LANES), 1)
    neg_a = -jnp.exp(alog_ref[...])
    dtb = dtb_ref[...]

    def conv_silu(x_ref, which, r0, ci):
        main = x_ref[pl.ds(r0, c), :]
        before = x_ref[pl.ds(jnp.maximum(r0 - halo, 0), halo), :]
        after = x_ref[pl.ds(jnp.minimum(r0 + c, seq - halo), halo), :]
        before = jnp.where(ci > 0, before, 0.0)
        after = jnp.where(ci < nch - 1, after, 0.0)
        win = jnp.concatenate([before, main, after], axis=0)
        n = c + 2 * halo
        y = jnp.zeros((c, PAIR), F32)
        for j in range(GDN_CONV):
            shifted = win if j == GDN_CONV // 2 else pltpu.roll(win, (GDN_CONV // 2 - j) % n, 0)
            y = y + shifted[halo:halo + c] * cw_ref[which, j:j + 1, :]
        return y * jax.nn.sigmoid(y)

    def l2n(x):
        sq = x * x
        ss = [jnp.sum(jnp.where(m, sq, 0.0), axis=1, keepdims=True) for m in head_mask]
        return x * lax.rsqrt(jnp.where(head_mask[0], ss[0], ss[1]) + NORM_EPS)

    ti2 = lax.broadcasted_iota(I32, (c, PAIR), 0)
    si2 = lax.broadcasted_iota(I32, (c, PAIR), 1) % HEAD_DIM
    incl2 = (si2 <= ti2, si2 >= ti2)
    strict2 = (si2 < ti2, si2 > ti2)
    eye2 = jnp.where(si2 == ti2, 1.0, 0.0)
    li = lax.broadcasted_iota(I32, (PAIR, PAIR), 0) // HEAD_DIM
    lj = lax.broadcasted_iota(I32, (PAIR, PAIR), 1) // HEAD_DIM
    same_head = li == lj

    def transform(it, carry):
        dirs = []
        for gi in range(GDN_GROUP):
            ci = it * GDN_GROUP + gi
            r0 = pl.multiple_of(ci * c, c)
            q = l2n(conv_silu(xq_ref, 0, r0, ci)) * (HEAD_DIM ** -0.5)
            k = l2n(conv_silu(xk_ref, 1, r0, ci))
            v = conv_silu(xv_ref, 2, r0, ci)
            ab = ab_ref[pl.ds(r0, c), :]
            gb = jnp.where(lane8 < 4, neg_a * jax.nn.softplus(ab + dtb), jax.nn.sigmoid(ab))
            g3 = _split3(gb)
            for d in range(2):
                dirs.append(dict(ci=ci, r0=r0, d=d, q=q, k=k, v=v, gb=gb, g3=g3))
        for dd in dirs:
            dd["gc"] = sum(_dot(tri_incl[dd["d"]], part) for part in dd["g3"])
        def stack(x):
            return jnp.concatenate([jnp.where(head_mask[0], x, 0.0), jnp.where(head_mask[1], x, 0.0)],
                                   axis=0).astype(BF16)

        for dd in dirs:
            d, gc, gb, k = dd["d"], dd["gc"], dd["gb"], dd["k"]
            c0 = 2 * d
            gct = gc.T
            g_last = gc[c - 1:c, :] if d == 0 else gc[0:1, :]
            gc_e = jnp.where(head_mask[0], gc[:, c0:c0 + 1], gc[:, c0 + 1:c0 + 2])
            beta_e = jnp.where(head_mask[0], gb[:, 4 + c0:5 + c0], gb[:, 5 + c0:6 + c0])
            gl_e = jnp.where(head_mask[0], g_last[:, c0:c0 + 1], g_last[:, c0 + 1:c0 + 2])
            gc_r = jnp.concatenate([gct[c0:c0 + 1, :], gct[c0 + 1:c0 + 2, :]], axis=1)
            dd["decay"] = jnp.where(incl2[d], jnp.exp(jnp.minimum(gc_e - gc_r, 0.0)), 0.0)
            kb = k * beta_e
            e_gc = jnp.exp(gc_e)
            dd.update(kb=kb.astype(BF16), kst=stack(k), e_gc=e_gc, erem=jnp.exp(gl_e - gc_e),
                      elast=jnp.exp(gl_e),
                      x0=jnp.concatenate([stack(dd["v"] * beta_e), stack(kb * e_gc)], axis=1))
        for dd in dirs:
            dd["a"] = _dot_nt(dd["kb"], dd["kst"])
        for dd in dirs:
            dd["qkr"] = _dot_nt(dd["q"].astype(BF16), dd["kst"])
        for dd in dirs:
            lmat = jnp.where(strict2[dd["d"]], dd["a"] * dd["decay"], 0.0)
            dd["pw"] = lmat.astype(BF16)
            dd["bd"] = stack(lmat)
            dd["t"] = eye2 - lmat
            dd["qk"] = jnp.where(incl2[dd["d"]], dd["qkr"] * dd["decay"], 0.0).astype(BF16)
        for dd in dirs:
            dd["sq"] = _dot(dd["pw"], dd["bd"])
        for dd in dirs:
            dd["pw"] = dd["sq"].astype(BF16)
            dd["bd"] = stack(dd["sq"])
        for _ in range(4):
            for dd in dirs:
                dd["tp"] = _dot(dd["t"].astype(BF16), dd["bd"])
            for dd in dirs:
                dd["sq"] = _dot(dd["pw"], dd["bd"])
            for dd in dirs:
                dd["t"] = dd["t"] + dd["tp"]
                dd["pw"] = dd["sq"].astype(BF16)
                dd["bd"] = stack(dd["sq"])
        for dd in dirs:
            dd["tp"] = _dot(dd["t"].astype(BF16), dd["bd"])
        for dd in dirs:
            dd["xu"] = _dot((dd["t"] + dd["tp"]).astype(BF16), dd["x0"])
        for dd in dirs:
            d, r0, ci = dd["d"], dd["r0"], dd["ci"]
            u_sum, w_sum = dd["xu"][:, :PAIR], dd["xu"][:, PAIR:]
            kd = (dd["k"] * dd["erem"]).astype(BF16)
            m_s[d, ci] = jnp.where(same_head, _dot_tn(kd, w_sum.astype(BF16)), 0.0).astype(BF16)
            n_s[d, ci] = jnp.where(same_head, _dot_tn(kd, u_sum.astype(BF16)), 0.0)
            qp_s[d, pl.ds(r0, c), :] = (dd["q"] * dd["e_gc"] - _dot(dd["qk"], stack(w_sum))).astype(BF16)
            op_s[d, pl.ds(r0, c), :] = _dot(dd["qk"], stack(u_sum))
            el_s[d, pl.ds(ci, 1), :] = dd["elast"]
        return carry

    lax.fori_loop(0, nch // GDN_GROUP, transform, 0)

    def scan(i, states):
        cis = (i, nch - 1 - i)
        rows = [pl.ds(pl.multiple_of(ci * c, c), c) for ci in cis]
        stb = [st.astype(BF16) for st in states]
        trans = [_dot(m_s[d, cis[d]], stb[d]) for d in range(2)]
        outs = [_dot(qp_s[d, rows[d], :], stb[d]) for d in range(2)]
        new_states = []
        for d in range(2):
            op_s[d, rows[d], :] += outs[d]
            new_states.append(states[d] * el_s[d, pl.ds(cis[d], 1), :] - trans[d] + n_s[d, cis[d]])
        return tuple(new_states)

    zero = jnp.zeros((PAIR, PAIR), F32)
    lax.fori_loop(0, nch, scan, (zero, zero))

    avg = avg_ref[...]
    tile = 512

    def finish(ti_, carry):
        r0 = pl.multiple_of(ti_ * tile, tile)
        o = op_s[0, pl.ds(r0, tile), :] + op_s[1, pl.ds(r0, tile), :]
        o = o * lax.rsqrt(_group_mean(o * o, avg) + NORM_EPS) * on_ref[...]
        gate = gate_ref[pl.ds(r0, tile), :]
        o_ref[pl.ds(r0, tile), :] = (o * (gate * jax.nn.sigmoid(gate))).astype(BF16)
        return carry

    lax.fori_loop(0, seq // tile, finish, 0)


def _gdn(pc, batch, seq, conv_w, a_log, dt_bias, o_norm_g):
    w = GDN_WIDTH
    pairs = w // PAIR
    pc3 = pc.reshape(batch, seq, pc.shape[-1])
    cw = conv_w.reshape(GDN_CONV, 3, w).transpose(1, 0, 2)
    cw = jnp.pad(cw, ((0, 0), (0, SUBLANES - GDN_CONV), (0, 0)))

    def pair_lanes(p):
        x = p.reshape(2, pairs, 2).transpose(1, 0, 2).reshape(pairs, 1, 4)
        return jnp.pad(x, ((0, 0), (0, 0), (0, LANES - 4)))

    nblk = w // PAIR

    def col(offset):
        return pl.BlockSpec((None, seq, PAIR), lambda b, hp: (b, 0, offset + hp))

    nch = seq // GDN_CHUNK
    return pl.pallas_call(
        _gdn_kernel,
        grid=(batch, pairs),
        in_specs=[col(0), col(nblk), col(2 * nblk), col(3 * nblk), col(4 * nblk),
                  pl.BlockSpec((3, SUBLANES, PAIR), lambda b, hp: (0, 0, hp)),
                  pl.BlockSpec((None, 1, LANES), lambda b, hp: (hp, 0, 0)),
                  pl.BlockSpec((None, 1, LANES), lambda b, hp: (hp, 0, 0)),
                  pl.BlockSpec((1, PAIR), lambda b, hp: (0, 0)),
                  pl.BlockSpec((PAIR, PAIR), lambda b, hp: (0, 0))],
        out_specs=pl.BlockSpec((None, seq, PAIR), lambda b, hp: (b, 0, hp)),
        out_shape=jax.ShapeDtypeStruct((batch, seq, w), BF16),
        scratch_shapes=[pltpu.VMEM((2, nch, PAIR, PAIR), BF16),
                        pltpu.VMEM((2, nch, PAIR, PAIR), F32),
                        pltpu.VMEM((2, seq, PAIR), BF16),
                        pltpu.VMEM((2, seq, PAIR), F32),
                        pltpu.VMEM((2, nch, PAIR), F32)],
        compiler_params=_cparams("parallel", "parallel"),
        name="gdn",
    )(pc3, pc3, pc3, pc3, pc3, cw, pair_lanes(a_log), pair_lanes(dt_bias),
      jnp.tile(o_norm_g, 2).reshape(1, PAIR),
      _block_avg_matrix(PAIR, HEAD_DIM)).reshape(batch * seq, w)


def _outproj_kernel(x_ref, ya_ref, yb_ref, yc_ref, wa_ref, wb_ref, wc_ref, g_ref, x1_ref, h_ref):
    x1 = (x_ref[...] + _dot(ya_ref[...], wa_ref[...]) + _dot(yb_ref[...], wb_ref[...])
          + _dot(yc_ref[...], wc_ref[...]))
    x1_ref[...] = x1
    h_ref[...] = (_rms(x1) * g_ref[...]).astype(BF16)


def _outproj(x2d, ya, yb, yc, w_out, g_ffn, tm=512):
    m, d = x2d.shape
    wa = w_out[:GMLP_WIDTH]
    wb = w_out[GMLP_WIDTH:GMLP_WIDTH + DSA_WIDTH]
    wc = w_out[GMLP_WIDTH + DSA_WIDTH:]
    const = lambda i: (0, 0)
    row = lambda wd: pl.BlockSpec((tm, wd), lambda i: (i, 0))
    return pl.pallas_call(
        _outproj_kernel,
        grid=(m // tm,),
        in_specs=[row(d), row(GMLP_WIDTH), row(DSA_WIDTH), row(GDN_WIDTH),
                  pl.BlockSpec(wa.shape, const), pl.BlockSpec(wb.shape, const),
                  pl.BlockSpec(wc.shape, const), pl.BlockSpec((1, d), const)],
        out_specs=[row(d), row(d)],
        out_shape=[jax.ShapeDtypeStruct((m, d), F32), jax.ShapeDtypeStruct((m, d), BF16)],
        compiler_params=_cparams("parallel"),
        name="outproj",
    )(x2d, ya, yb, yc, wa, wb, wc, g_ffn.reshape(1, d))


def _router_kernel(h_ref, wr_ref, tri_ref, pos_ref, aff_ref, offs_ref, bits_s, *, cap):
    seq = h_ref.shape[0]
    nblk = seq // LANES
    logits = _dot_nt(wr_ref[...], h_ref[...])
    ex = jnp.exp(logits - jnp.max(logits, axis=0, keepdims=True))
    aff = ex / jnp.sum(ex, axis=0, keepdims=True)
    aff_ref[...] = aff
    bits = lax.bitcast_convert_type(aff, I32)
    bits_s[...] = bits

    def bisect(it, prefix):
        cand = prefix | jnp.left_shift(jnp.int32(1), 30 - it)
        cnt = jnp.sum(jnp.where(bits_s[...] >= cand, 1.0, 0.0), axis=1, keepdims=True)
        return jnp.where(cnt >= cap, cand, prefix)

    thr = lax.fori_loop(0, 31, bisect, jnp.zeros((N_EXPERTS, 1), I32))
    n_gt = jnp.sum(jnp.where(bits > thr, 1.0, 0.0), axis=1, keepdims=True)
    need = cap - n_gt
    tri = tri_ref[...]
    lane = lax.broadcasted_iota(I32, (N_EXPERTS, LANES), 1)

    off_eq = jnp.zeros((N_EXPERTS, 1), F32)
    off_sel = jnp.zeros((N_EXPERTS, 1), F32)
    offs = jnp.zeros((N_EXPERTS, LANES), F32)
    for j in range(nblk):
        cols = slice(j * LANES, (j + 1) * LANES)
        bj = bits_s[:, cols]
        eq = bj == thr
        eq_f = jnp.where(eq, 1.0, 0.0)
        rank = off_eq + _dot(eq_f.astype(BF16), tri)
        sel = (bj > thr) | (eq & (rank < need))
        sel_f = jnp.where(sel, 1.0, 0.0)
        pos = off_sel + _dot(sel_f.astype(BF16), tri)
        pos_ref[:, cols] = jnp.where(sel, pos, -1.0)
        offs = jnp.where(lane == j, off_sel, offs)
        off_eq = off_eq + jnp.sum(eq_f, axis=1, keepdims=True)
        off_sel = off_sel + jnp.sum(sel_f, axis=1, keepdims=True)
    offs_ref[...] = jnp.where(lane == nblk, off_sel, offs).astype(I32)


def _router(h3, w_router):
    batch, seq, d = h3.shape
    cap = EC_CAPACITY * seq // N_EXPERTS
    i = jnp.arange(LANES)
    tri = jnp.where(i[:, None] < i[None, :], 1.0, 0.0).astype(BF16)
    out3 = lambda wd: pl.BlockSpec((None, N_EXPERTS, wd), lambda b: (b, 0, 0))
    return pl.pallas_call(
        functools.partial(_router_kernel, cap=cap),
        grid=(batch,),
        in_specs=[pl.BlockSpec((None, seq, d), lambda b: (b, 0, 0)),
                  pl.BlockSpec((N_EXPERTS, d), lambda b: (0, 0)),
                  pl.BlockSpec((LANES, LANES), lambda b: (0, 0))],
        out_specs=[out3(seq), out3(seq), out3(LANES)],
        out_shape=[jax.ShapeDtypeStruct((batch, N_EXPERTS, seq), F32),
                   jax.ShapeDtypeStruct((batch, N_EXPERTS, seq), F32),
                   jax.ShapeDtypeStruct((batch, N_EXPERTS, LANES), I32)],
        scratch_shapes=[pltpu.VMEM((N_EXPERTS, seq), I32)],
        compiler_params=_cparams("parallel"),
        name="router",
    )(h3, w_router.T.astype(BF16), tri)


def _moe_ffn_kernel(offs_ref, h_ref, pos_ref, aff_ref, wg_ref, wu_ref, wd_ref, ye_ref, x_s, gate_s, w_s, *, cap):
    nblk, tblk = pos_ref.shape
    per = tblk // LANES
    e = pl.program_id(0)
    b = pl.program_id(1)
    base = (b * N_EXPERTS + e) * OFFS_W

    @pl.when(b == 0)
    def _():
        w_s[0] = wg_ref[...].astype(BF16)
        w_s[1] = wu_ref[...].astype(BF16)
        w_s[2] = wd_ref[...].astype(BF16)

    rid = lax.broadcasted_iota(I32, (LANES, tblk), 0)

    def contribution(j, want):
        hit = pos_ref[pl.ds(j, 1), :] == want
        hj = h_ref[pl.ds(pl.multiple_of(j * tblk, tblk), tblk), :]
        return (_dot(jnp.where(hit, 1.0, 0.0).astype(BF16), hj),
                jnp.sum(jnp.where(hit, aff_ref[pl.ds(j, 1), :], 0.0), axis=1, keepdims=True))

    for c in range(cap // LANES):
        rows = slice(c * LANES, (c + 1) * LANES)
        first = jnp.int32(0)
        for j in range(nblk):
            first = first + (offs_ref[base + (j + 1) * per] <= c * LANES).astype(I32)
        x = jnp.zeros((LANES, h_ref.shape[1]), F32)
        gate = jnp.zeros((LANES, 1), F32)
        for k in range(MOE_STATIC):
            j = first + k
            shift = jnp.where(j < nblk, c * LANES, -cap)
            dx, dg = contribution(jnp.minimum(j, nblk - 1), (rid + shift).astype(F32))
            x, gate = x + dx, gate + dg
        x_s[rows, :] = x
        gate_s[rows, :] = jnp.broadcast_to(gate, (LANES, LANES))
        want = (rid + c * LANES).astype(F32)

        def rest(j, carry):
            @pl.when(offs_ref[base + j * per] < (c + 1) * LANES)
            def _():
                dx, dg = contribution(j, want)
                x_s[rows, :] += dx
                gate_s[rows, :] += dg

            return carry

        lax.fori_loop(first + MOE_STATIC, nblk, rest, 0)

    half = cap // 2
    for r in range(2):
        rows = slice(r * half, (r + 1) * half)
        x = x_s[rows, :].astype(BF16)
        g = _dot(x, w_s[0])
        u = _dot(x, w_s[1])
        hid = (g * jax.nn.sigmoid(g) * u).astype(BF16)
        ye_ref[rows, :] = (_dot(hid, w_s[2]) * gate_s[rows, 0:1]).astype(BF16)


def _moe_ffn(offs_flat, h3, pos4, aff4, wg, wu, wd, layer):
    batch, seq, d = h3.shape
    cap = EC_CAPACITY * seq // N_EXPERTS
    nblk, tblk = pos4.shape[2:]
    row = pl.BlockSpec((None, None, nblk, tblk), lambda e, b, o: (b, e, 0, 0))
    wspec = pl.BlockSpec((None, None, d, d), lambda e, b, o: (layer, e, 0, 0))
    return pl.pallas_call(
        functools.partial(_moe_ffn_kernel, cap=cap),
        grid_spec=pltpu.PrefetchScalarGridSpec(
            num_scalar_prefetch=1,
            grid=(N_EXPERTS, batch),
            in_specs=[pl.BlockSpec((None, seq, d), lambda e, b, o: (b, 0, 0)), row, row,
                      wspec, wspec, wspec],
            out_specs=pl.BlockSpec((None, None, cap, d), lambda e, b, o: (b, e, 0, 0)),
            scratch_shapes=[pltpu.VMEM((cap, d), F32), pltpu.VMEM((cap, LANES), F32),
                            pltpu.VMEM((3, d, d), BF16)]),
        out_shape=jax.ShapeDtypeStruct((batch, N_EXPERTS, cap, d), BF16),
        compiler_params=_cparams("arbitrary", "arbitrary"),
        name="moe_ffn",
    )(offs_flat, h3, pos4, aff4, wg, wu, wd)


def _moe_combine_kernel(offs_ref, x1_ref, ye_ref, ptok_ref, p_ref, wple_ref, wpg_ref, gple_ref, gpg_ref,
                        o_ref, acc_s, *, cap):
    tb = x1_ref.shape[0]
    b = pl.program_id(0)
    jb = pl.program_id(1)
    nsub = tb // LANES
    pack = 2 * SUBLANES

    def offsets(sub, e):
        at = (b * N_EXPERTS + e) * OFFS_W + jb * nsub + sub
        return offs_ref[at], offs_ref[at + 1]

    def scatter(win, align, group):
        cid = lax.broadcasted_iota(I32, (LANES, win), 1)
        for sub in range(nsub):
            rows = slice(sub * LANES, (sub + 1) * LANES)
            acc = x1_ref[rows, :]
            for e0 in range(0, N_EXPERTS, group):
                hits, wins = [], []
                for e in range(e0, e0 + group):
                    lo, _ = offsets(sub, e)
                    start = pl.multiple_of(jnp.minimum(_floor_to(lo, align), cap - win), align)
                    hit = ptok_ref[rows, e:e + 1] == (cid + start).astype(F32)
                    hits.append(jnp.where(hit, 1.0, 0.0).astype(BF16))
                    wins.append(ye_ref[e, pl.ds(start, win), :])
                acc = acc + _dot(jnp.concatenate(hits, axis=1), jnp.concatenate(wins, axis=0))
            acc_s[rows, :] = acc

    narrow = jnp.bool_(True)
    for sub in range(nsub):
        for e in range(N_EXPERTS):
            lo, hi = offsets(sub, e)
            narrow = narrow & (hi - _floor_to(lo, pack) <= LANES)

    @pl.when(narrow)
    def _():
        scatter(LANES, pack, 2)

    @pl.when(jnp.logical_not(narrow))
    def _():
        scatter(2 * LANES, LANES, 1)

    x2 = acc_s[...]
    emb = _rms(_dot(p_ref[...].astype(BF16), wple_ref[...])) * gple_ref[...]
    gate = jax.nn.sigmoid(_dot((_rms(x2) * gpg_ref[...]).astype(BF16), wpg_ref[...]))
    o_ref[...] = x2 + emb * gate


def _moe_combine(offs_flat, x1_3, ye, ptok, p4, layer, w_ple, w_pg, g_ple, g_pg, tb=256):
    batch, seq, d = x1_3.shape
    cap = ye.shape[2]
    pd = p4.shape[-1]
    const = lambda b, j, o: (0, 0)
    row = lambda wd: pl.BlockSpec((None, tb, wd), lambda b, j, o: (b, j, 0))
    return pl.pallas_call(
        functools.partial(_moe_combine_kernel, cap=cap),
        grid_spec=pltpu.PrefetchScalarGridSpec(
            num_scalar_prefetch=1,
            grid=(batch, seq // tb),
            in_specs=[row(d),
                      pl.BlockSpec((None, N_EXPERTS, cap, d), lambda b, j, o: (b, 0, 0, 0)),
                      row(N_EXPERTS),
                      pl.BlockSpec((None, None, tb, pd), lambda b, j, o: (layer, b, j, 0)),
                      pl.BlockSpec((pd, d), const), pl.BlockSpec((d, d), const),
                      pl.BlockSpec((1, d), const), pl.BlockSpec((1, d), const)],
            out_specs=row(d),
            scratch_shapes=[pltpu.VMEM((tb, d), F32)]),
        out_shape=jax.ShapeDtypeStruct((batch, seq, d), F32),
        compiler_params=_cparams("arbitrary", "arbitrary"),
        name="moe_combine_ple",
    )(offs_flat, x1_3, ye, ptok, p4, w_ple, w_pg, g_ple.reshape(1, d), g_pg.reshape(1, d))


def _layer(x2d, p4, tables, batch, seq, prm):
    d = x2d.shape[-1]
    pa, pb, pc = _inproj(x2d, prm["g_mix"].reshape(1, d), prm["w_in"],
                         (2 * GMLP_WIDTH, 3 * DSA_WIDTH, prm["w_in"].shape[1] - 2 * GMLP_WIDTH - 3 * DSA_WIDTH))
    ya = _gmlp(pa, prm["ln_v_g"], prm["ln_v_b"], prm["w_s"], prm["b_s"])
    yb = _dsa(_dsa_prep(pb, tables, batch, seq, prm["q_norm_g"], prm["k_norm_g"]), batch, seq)
    yc = _gdn(pc, batch, seq, prm["conv_w"], prm["a_log"], prm["dt_bias"], prm["o_norm_g"])
    x1, h = _outproj(x2d, ya, yb, yc, prm["w_out"], prm["g_ffn"])
    h3 = h.reshape(batch, seq, d)
    pos, aff, offs = _router(h3, prm["w_router"])
    offs_flat = offs[:, :, :OFFS_W].reshape(-1)
    nblk = seq // MOE_TBLK
    ye = _moe_ffn(offs_flat, h3, pos.reshape(batch, N_EXPERTS, nblk, MOE_TBLK),
                  aff.reshape(batch, N_EXPERTS, nblk, MOE_TBLK),
                  prm["w_e_gate"], prm["w_e_up"], prm["w_e_down"], prm["layer"])
    x2 = _moe_combine(offs_flat, x1.reshape(batch, seq, d), ye, pos.transpose(0, 2, 1), p4, prm["layer"],
                      prm["w_ple"], prm["w_ple_gate"], prm["g_ple"], prm["g_ple_gate"])
    return x2.reshape(batch * seq, d)


def _arrange_in_weights(w_in):
    heads = GDN_WIDTH // HEAD_DIM
    pairs = GDN_WIDTH // PAIR
    first = w_in.shape[-1] - 4 * heads
    cols = [first + which * 2 * heads + d * heads + 2 * hp + hh
            for hp in range(pairs) for which in range(2) for d in range(2) for hh in range(2)]
    gates = w_in[:, :, jnp.array(cols)].reshape(w_in.shape[:2] + (pairs, 8))
    gates = jnp.pad(gates, ((0, 0), (0, 0), (0, 0), (0, LANES - 8))).reshape(w_in.shape[:2] + (pairs * LANES,))
    return jnp.concatenate([w_in[:, :, :first], gates], axis=-1).astype(BF16)


def kernel(x, p, positions, g_mix, w_in, ln_v_g, ln_v_b, w_s, b_s, q_norm_g, k_norm_g, conv_w, a_log, dt_bias,
           o_norm_g, w_out, g_ffn, w_router, w_e_gate, w_e_up, w_e_down, w_ple, g_ple, g_ple_gate, w_ple_gate):
    batch, seq, d = x.shape
    depth = p.shape[0]
    w_in_b = _arrange_in_weights(w_in)
    tables = _rope_tables(positions)
    x2d = x.reshape(batch * seq, d)
    for i in range(depth):
        prm = dict(g_mix=g_mix[i], w_in=w_in_b[i], ln_v_g=ln_v_g[i], ln_v_b=ln_v_b[i], w_s=w_s[i], b_s=b_s[i],
                   q_norm_g=q_norm_g[i], k_norm_g=k_norm_g[i], conv_w=conv_w[i], a_log=a_log[i],
                   dt_bias=dt_bias[i], o_norm_g=o_norm_g[i], w_out=w_out[i].astype(BF16), g_ffn=g_ffn[i],
                   w_router=w_router[i], w_e_gate=w_e_gate, w_e_up=w_e_up, w_e_down=w_e_down, layer=i,
                   w_ple=w_ple[i].astype(BF16), g_ple=g_ple[i],
                   g_ple_gate=g_ple_gate[i], w_ple_gate=w_ple_gate[i].astype(BF16))
        x2d = _layer(x2d, p, tables, batch, seq, prm)
    return x2d.reshape(batch, seq, d)
```

```python
import functools

import jax
import jax.numpy as jnp
from jax import lax
from jax.experimental import pallas as pl
from jax.experimental.pallas import tpu as pltpu

F32 = jnp.float32
BF16 = jnp.bfloat16
I32 = jnp.int32

HEAD_DIM = 64
GMLP_WIDTH = 256
GMLP_GROUPS = 4
GMLP_CHUNK = 128
DSA_WIDTH = 384
DSA_PATTERNS = ((128, 1), (512, 4), (2048, 16))
GDN_WIDTH = 384
GDN_CONV = 5
ROT_DIM = 16
ROPE_THETA = 500000.0
N_EXPERTS = 16
EC_CAPACITY = 2
NORM_EPS = 1e-6
MASK_VALUE = -1e30

LANES = 128
SUBLANES = 8
VMEM_LIMIT_BYTES = 56 * 1024 * 1024

PAIR = 2 * HEAD_DIM
GDN_CHUNK = 64
GDN_GROUP = 8
ATT_HALO = 64
ATT_SUB = 128
OFFS_W = 40
MOE_TBLK = 256
MOE_STATIC = 6


def _cparams(*sem):
    return pltpu.CompilerParams(dimension_semantics=sem, vmem_limit_bytes=VMEM_LIMIT_BYTES)


def _dot(a, b):
    return jnp.dot(a, b, preferred_element_type=F32)


def _dot_nt(a, b):
    return lax.dot_general(a, b, (((1,), (1,)), ((), ())), preferred_element_type=F32)


def _dot_tn(a, b):
    return lax.dot_general(a, b, (((0,), (0,)), ((), ())), preferred_element_type=F32)


def _split2(x):
    hi = x.astype(BF16)
    lo = (x - hi.astype(F32)).astype(BF16)
    return hi, lo


def _split3(x):
    hi = x.astype(BF16)
    r = x - hi.astype(F32)
    mid = r.astype(BF16)
    lo = (r - mid.astype(F32)).astype(BF16)
    return hi, mid, lo


def _group_mean(x, avg):
    hi, lo = _split2(x)
    return _dot(hi, avg) + _dot(lo, avg)


def _rms(x):
    return x * lax.rsqrt(jnp.mean(x * x, axis=-1, keepdims=True) + NORM_EPS)


def _floor_to(x, pow2):
    s = pow2.bit_length() - 1
    return lax.shift_left(lax.shift_right_logical(x, s), s)


def _block_avg_matrix(width, group):
    i = jnp.arange(width)
    return jnp.where((i[:, None] // group) == (i[None, :] // group), 1.0 / group, 0.0).astype(BF16)


def _inproj_kernel(x_ref, g_ref, w_ref, *o_refs):
    x = x_ref[...]
    xn = (_rms(x) * g_ref[...]).astype(BF16)
    col = 0
    for o_ref in o_refs:
        n = o_ref.shape[-1]
        o_ref[...] = _dot(xn, w_ref[:, col:col + n])
        col += n


def _inproj(x2d, g, w, widths, tm=512):
    m, d = x2d.shape
    n = w.shape[1]
    return pl.pallas_call(
        _inproj_kernel,
        grid=(m // tm,),
        in_specs=[pl.BlockSpec((tm, d), lambda i: (i, 0)),
                  pl.BlockSpec((1, d), lambda i: (0, 0)),
                  pl.BlockSpec((d, n), lambda i: (0, 0))],
        out_specs=[pl.BlockSpec((tm, wd), lambda i: (i, 0)) for wd in widths],
        out_shape=[jax.ShapeDtypeStruct((m, wd), F32) for wd in widths],
        compiler_params=_cparams("parallel"),
        name="inproj",
    )(x2d, g, w)


def _gmlp_kernel(u_ref, v_ref, lng_ref, lnb_ref, ws_ref, bias_ref, avg_ref, o_ref):
    tm = u_ref.shape[0]
    u = jax.nn.gelu(u_ref[...])
    vf = jax.nn.gelu(v_ref[...])
    avg = avg_ref[...]
    mu = _group_mean(vf, avg)
    dv = vf - mu
    var = _group_mean(dv * dv, avg)
    vn = dv * lax.rsqrt(var + NORM_EPS) * lng_ref[...] + lnb_ref[...]
    grp = lax.broadcasted_iota(I32, (GMLP_CHUNK, GMLP_WIDTH), 1) // HEAD_DIM
    for c in range(tm // GMLP_CHUNK):
        rows = slice(c * GMLP_CHUNK, (c + 1) * GMLP_CHUNK)
        vc = vn[rows].astype(BF16)
        mixed = bias_ref[...]
        for g in range(GMLP_GROUPS):
            mixed = mixed + jnp.where(grp == g, _dot(ws_ref[g], vc), 0.0)
        o_ref[rows, :] = (u[rows] * mixed).astype(BF16)


def _gmlp(pa, ln_g, ln_b, w_s, b_s, tm=512):
    m = pa.shape[0]
    w = GMLP_WIDTH
    bias2d = jnp.repeat(b_s.T, HEAD_DIM, axis=1)
    const = lambda i: (0, 0)
    return pl.pallas_call(
        _gmlp_kernel,
        grid=(m // tm,),
        in_specs=[pl.BlockSpec((tm, w), lambda i: (i, 0)),
                  pl.BlockSpec((tm, w), lambda i: (i, 1)),
                  pl.BlockSpec((1, w), const),
                  pl.BlockSpec((1, w), const),
                  pl.BlockSpec((GMLP_GROUPS, GMLP_CHUNK, GMLP_CHUNK), lambda i: (0, 0, 0)),
                  pl.BlockSpec((GMLP_CHUNK, w), const),
                  pl.BlockSpec((w, w), const)],
        out_specs=pl.BlockSpec((tm, w), lambda i: (i, 0)),
        out_shape=jax.ShapeDtypeStruct((m, w), BF16),
        compiler_params=_cparams("parallel"),
        name="gmlp",
    )(pa, pa, ln_g.reshape(1, w), ln_b.reshape(1, w), w_s.astype(BF16), bias2d,
      _block_avg_matrix(w, HEAD_DIM))


def _rope_table_kernel(pos_ref, invf_ref, sa_ref, sb_ref, cos_ref, sina_ref, sinb_ref):
    ang = pos_ref[...].astype(F32) * invf_ref[...]
    s = jnp.sin(ang)
    cos_ref[...] = jnp.cos(ang)
    sina_ref[...] = s * sa_ref[...]
    sinb_ref[...] = s * sb_ref[...]


def _rope_tables(positions, ts=1024):
    m = positions.size
    half = ROT_DIM // 2
    inv_freq = ROPE_THETA ** (-jnp.arange(half, dtype=F32) * 2.0 / ROT_DIM)
    lane = jnp.arange(PAIR) % HEAD_DIM
    invf = jnp.where(lane < ROT_DIM, inv_freq[lane % half], 0.0).reshape(1, PAIR)
    sa = jnp.where(lane < half, -1.0, 0.0).astype(F32).reshape(1, PAIR)
    sb = jnp.where((lane >= half) & (lane < ROT_DIM), 1.0, 0.0).astype(F32).reshape(1, PAIR)
    const = lambda i: (0, 0)
    row = pl.BlockSpec((ts, PAIR), lambda i: (i, 0))
    return pl.pallas_call(
        _rope_table_kernel,
        grid=(m // ts,),
        in_specs=[pl.BlockSpec((ts, 1), lambda i: (i, 0)),
                  pl.BlockSpec((1, PAIR), const), pl.BlockSpec((1, PAIR), const),
                  pl.BlockSpec((1, PAIR), const)],
        out_specs=[row, row, row],
        out_shape=[jax.ShapeDtypeStruct((m, PAIR), F32)] * 3,
        compiler_params=_cparams("parallel"),
        name="rope_tables",
    )(positions.reshape(m, 1), invf, sa, sb)


def _dsa_prep_kernel(q_ref, k_ref, v_ref, cos_ref, sina_ref, sinb_ref, gq_ref, gk_ref, avg_ref, *refs):
    o_refs, stage = refs[:-1], refs[-1]
    tm = q_ref.shape[0]
    w = DSA_WIDTH
    cos, sina, sinb = cos_ref[...], sina_ref[...], sinb_ref[...]
    avg = avg_ref[...]
    half = ROT_DIM // 2

    def norm_rot(t, g, scale, col0):
        t = t * lax.rsqrt(_group_mean(t * t, avg) + NORM_EPS) * g
        for hp in range(w // PAIR):
            tp = t[:, hp * PAIR:(hp + 1) * PAIR]
            rot = tp * cos + pltpu.roll(tp, PAIR - half, 1) * sina + pltpu.roll(tp, half, 1) * sinb
            stage[col0 + hp] = rot * scale

    npairs = w // PAIR
    norm_rot(q_ref[...], gq_ref[...], HEAD_DIM ** -0.5, 0)
    norm_rot(k_ref[...], gk_ref[...], 1.0, npairs)
    for hp in range(npairs):
        stage[2 * npairs + hp] = v_ref[:, hp * PAIR:(hp + 1) * PAIR]
    for o_ref, (_, dil) in zip(o_refs, DSA_PATTERNS):
        for r in range(dil):
            rows = pl.ds(r, tm // dil, stride=dil) if dil > 1 else slice(None)
            for g in range(3 * npairs):
                o_ref[r, :, g * PAIR:(g + 1) * PAIR] = stage[g, rows, :].astype(BF16)


def _dsa_prep(pb, tables, batch, seq, q_norm_g, k_norm_g, tm=512):
    w = DSA_WIDTH
    const = lambda b, i: (0, 0)
    pb3 = pb.reshape(batch, seq, 3 * w)
    tab = pl.BlockSpec((None, tm, PAIR), lambda b, i: (b, i, 0))
    gq = jnp.tile(q_norm_g, w // HEAD_DIM).reshape(1, w)
    gk = jnp.tile(k_norm_g, w // HEAD_DIM).reshape(1, w)
    dils = [dil for _, dil in DSA_PATTERNS]
    return pl.pallas_call(
        _dsa_prep_kernel,
        grid=(batch, seq // tm),
        in_specs=[pl.BlockSpec((None, tm, w), lambda b, i: (b, i, 0)),
                  pl.BlockSpec((None, tm, w), lambda b, i: (b, i, 1)),
                  pl.BlockSpec((None, tm, w), lambda b, i: (b, i, 2)),
                  tab, tab, tab,
                  pl.BlockSpec((1, w), const), pl.BlockSpec((1, w), const),
                  pl.BlockSpec((w, w), const)],
        out_specs=[pl.BlockSpec((None, dil, tm // dil, 3 * w), lambda b, i: (b, 0, i, 0)) for dil in dils],
        out_shape=[jax.ShapeDtypeStruct((batch, dil, seq // dil, 3 * w), BF16) for dil in dils],
        scratch_shapes=[pltpu.VMEM((3 * w // PAIR, tm, PAIR), F32)],
        compiler_params=_cparams("parallel", "parallel"),
        name="dsa_prep",
    )(pb3, pb3, pb3, *[t.reshape(batch, seq, PAIR) for t in tables], gq, gk, _block_avg_matrix(w, HEAD_DIM))


def _dsa_kernel(q_ref, kp_ref, k_ref, kn_ref, vp_ref, v_ref, vn_ref, o_ref, lse_ref, kbuf, vtbuf, *,
                steps, sub_len):
    t = q_ref.shape[0]
    i = pl.program_id(2)
    kbuf[0:ATT_HALO, :] = kp_ref[...]
    kbuf[ATT_HALO:ATT_HALO + t, :] = k_ref[...]
    kbuf[ATT_HALO + t:, :] = kn_ref[...]
    npairs = DSA_WIDTH // PAIR
    vfull = jnp.concatenate([vp_ref[...], v_ref[...], vn_ref[...]], axis=0).astype(F32)
    for hp in range(npairs):
        vtbuf[hp] = vfull[:, hp * PAIR:(hp + 1) * PAIR].T.astype(BF16)

    nk = ATT_SUB + 2 * ATT_HALO
    lane = lax.broadcasted_iota(I32, (1, PAIR), 1)
    lane_mask = (lane < HEAD_DIM, lane >= HEAD_DIM)
    row = lax.broadcasted_iota(I32, (PAIR, 1), 0)
    row_mask = (row < HEAD_DIM, row >= HEAD_DIM)
    stat_row = lax.broadcasted_iota(I32, (LANES, ATT_SUB), 0)
    for s in range(t // ATT_SUB):
        rows = slice(s * ATT_SUB, (s + 1) * ATT_SUB)
        base = i * t + s * ATT_SUB
        kj = base - ATT_HALO + lax.broadcasted_iota(I32, (nk, 1), 0)
        qi = base + lax.broadcasted_iota(I32, (1, ATT_SUB), 1)
        valid = (jnp.abs(kj - qi) <= steps) & (kj >= 0) & (kj < sub_len)
        kk = kbuf[s * ATT_SUB:s * ATT_SUB + nk, :]
        stats = jnp.zeros((LANES, ATT_SUB), F32)
        for hp in range(npairs):
            cols = slice(hp * PAIR, (hp + 1) * PAIR)
            qp = q_ref[rows, cols]
            kp = kk[:, cols]
            vt = vtbuf[hp, :, s * ATT_SUB:s * ATT_SUB + nk]
            acc_t = jnp.zeros((PAIR, ATT_SUB), F32)
            for hh in range(2):
                km = jnp.where(lane_mask[hh], kp, jnp.zeros_like(kp))
                sc = jnp.where(valid, _dot_nt(km, qp), MASK_VALUE)
                m = jnp.max(sc, axis=0, keepdims=True)
                p = jnp.exp(sc - m)
                l = jnp.sum(p, axis=0, keepdims=True)
                vth = jnp.where(row_mask[hh], vt, jnp.zeros_like(vt))
                acc_t = acc_t + _dot(vth, p.astype(BF16)) * (1.0 / l)
                stats = jnp.where(stat_row == 2 * hp + hh, m + jnp.log(l), stats)
            o_ref[rows, cols] = acc_t.T.astype(BF16)
        lse_ref[rows, :] = stats.T


def _dsa_pattern(qkv, window, dil):
    batch, _, sub_len, _ = qkv.shape
    w = DSA_WIDTH
    steps = window // (2 * dil)
    assert steps <= ATT_HALO
    t = min(1024, sub_len)
    nb64 = sub_len // ATT_HALO
    r64 = t // ATT_HALO

    def main(which):
        return pl.BlockSpec((None, None, t, w), lambda b, r, i: (b, r, i, which))

    def prev(which):
        return pl.BlockSpec((None, None, ATT_HALO, w),
                            lambda b, r, i: (b, r, jnp.maximum(i * r64 - 1, 0), which))

    def nxt(which):
        return pl.BlockSpec((None, None, ATT_HALO, w),
                            lambda b, r, i: (b, r, jnp.minimum((i + 1) * r64, nb64 - 1), which))

    out = lambda wd: pl.BlockSpec((None, None, t, wd), lambda b, r, i: (b, r, i, 0))
    return pl.pallas_call(
        functools.partial(_dsa_kernel, steps=steps, sub_len=sub_len),
        grid=(batch, dil, sub_len // t),
        in_specs=[main(0), prev(1), main(1), nxt(1), prev(2), main(2), nxt(2)],
        out_specs=[out(w), out(LANES)],
        out_shape=[jax.ShapeDtypeStruct((batch, dil, sub_len, w), BF16),
                   jax.ShapeDtypeStruct((batch, dil, sub_len, LANES), F32)],
        scratch_shapes=[pltpu.VMEM((t + 2 * ATT_HALO, w), BF16),
                        pltpu.VMEM((w // PAIR, PAIR, t + 2 * ATT_HALO), BF16)],
        compiler_params=_cparams("parallel", "parallel", "parallel"),
        name=f"dsa_d{dil}",
    )(*([qkv] * 7))


def _dsa_combine_kernel(*refs):
    n = len(DSA_PATTERNS)
    o_refs, lse_refs, expand_ref, y_ref = refs[:n], refs[n:2 * n], refs[2 * n], refs[2 * n + 1]
    stages = iter(refs[2 * n + 2:])
    tm = y_ref.shape[0]
    npairs = DSA_WIDTH // PAIR
    outs, lses = [], []
    for o_ref, lse_ref, (_, dil) in zip(o_refs, lse_refs, DSA_PATTERNS):
        if dil == 1:
            outs.append([o_ref[0, :, hp * PAIR:(hp + 1) * PAIR].astype(F32) for hp in range(npairs)])
            lses.append(lse_ref[0])
        else:
            so, sl = next(stages), next(stages)
            for r in range(dil):
                rows = pl.ds(r, tm // dil, stride=dil)
                for hp in range(npairs):
                    so[hp, rows, :] = o_ref[r, :, hp * PAIR:(hp + 1) * PAIR].astype(F32)
                sl[rows, :] = lse_ref[r]
            outs.append([so[hp] for hp in range(npairs)])
            lses.append(sl[...])
    top = functools.reduce(jnp.maximum, lses)
    es = [jnp.exp(lse - top) for lse in lses]
    inv = 1.0 / sum(es)
    weights = [_group_mean(e * inv, expand_ref[...]) for e in es]
    for hp in range(npairs):
        cols = slice(hp * PAIR, (hp + 1) * PAIR)
        y_ref[:, cols] = sum(o[hp] * wgt[:, cols] for o, wgt in zip(outs, weights)).astype(BF16)


def _dsa_combine(outs, lses, batch, seq, tm=512):
    w = DSA_WIDTH
    dils = [dil for _, dil in DSA_PATTERNS]
    lane = jnp.arange(w) // HEAD_DIM
    expand = (jnp.arange(LANES)[:, None] == lane[None, :]).astype(BF16)
    spec = lambda dil, wd: pl.BlockSpec((None, dil, tm // dil, wd), lambda b, i: (b, 0, i, 0))
    scratch = []
    for dil in dils:
        if dil > 1:
            scratch += [pltpu.VMEM((w // PAIR, tm, PAIR), F32), pltpu.VMEM((tm, LANES), F32)]
    return pl.pallas_call(
        _dsa_combine_kernel,
        grid=(batch, seq // tm),
        in_specs=[spec(dil, w) for dil in dils] + [spec(dil, LANES) for dil in dils]
                 + [pl.BlockSpec((LANES, w), lambda b, i: (0, 0))],
        out_specs=pl.BlockSpec((None, tm, w), lambda b, i: (b, i, 0)),
        out_shape=jax.ShapeDtypeStruct((batch, seq, w), BF16),
        scratch_shapes=scratch,
        compiler_params=_cparams("parallel", "parallel"),
        name="dsa_combine",
    )(*outs, *lses, expand).reshape(batch * seq, w)


def _dsa(qkvs, batch, seq):
    res = [_dsa_pattern(qkv, window, dil) for qkv, (window, dil) in zip(qkvs, DSA_PATTERNS)]
    return _dsa_combine([r[0] for r in res], [r[1] for r in res], batch, seq)


def _gdn_kernel(xq_ref, xk_ref, xv_ref, gate_ref, ab_ref, cw_ref, alog_ref, dtb_ref, on_ref,
                avg_ref, o_ref, m_s, n_s, qp_s, op_s, el_s):
    seq = xq_ref.shape[0]
    c = GDN_CHUNK
    nch = seq // c
    halo = SUBLANES
    lane = lax.broadcasted_iota(I32, (1, PAIR), 1)
    head_mask = (lane < HEAD_DIM, lane >= HEAD_DIM)
    ti = lax.broadcasted_iota(I32, (c, c), 0)
    si = lax.broadcasted_iota(I32, (c, c), 1)
    incl = (si <= ti, si >= ti)
    tri_incl = tuple(jnp.where(m, 1.0, 0.0).astype(BF16) for m in incl)
    lane8 = lax.broadcasted_iota(I32, (1, LANES), 1)
    neg_a = -jnp.exp(alog_ref[...])
    dtb = dtb_ref[...]

    def conv_silu(x_ref, which, r0, ci):
        main = x_ref[pl.ds(r0, c), :]
        before = x_ref[pl.ds(jnp.maximum(r0 - halo, 0), halo), :]
        after = x_ref[pl.ds(jnp.minimum(r0 + c, seq - halo), halo), :]
        before = jnp.where(ci > 0, before, 0.0)
        after = jnp.where(ci < nch - 1, after, 0.0)
        win = jnp.concatenate([before, main, after], axis=0)
        n = c + 2 * halo
        y = jnp.zeros((c, PAIR), F32)
        for j in range(GDN_CONV):
            shifted = win if j == GDN_CONV // 2 else pltpu.roll(win, (GDN_CONV // 2 - j) % n, 0)
            y = y + shifted[halo:halo + c] * cw_ref[which, j:j + 1, :]
        return y * jax.nn.sigmoid(y)

    def l2n(x):
        sq = x * x
        ss = [jnp.sum(jnp.where(m, sq, 0.0), axis=1, keepdims=True) for m in head_mask]
        return x * lax.rsqrt(jnp.where(head_mask[0], ss[0], ss[1]) + NORM_EPS)

    ti2 = lax.broadcasted_iota(I32, (c, PAIR), 0)
    si2 = lax.broadcasted_iota(I32, (c, PAIR), 1) % HEAD_DIM
    incl2 = (si2 <= ti2, si2 >= ti2)
    strict2 = (si2 < ti2, si2 > ti2)
    eye2 = jnp.where(si2 == ti2, 1.0, 0.0)
    li = lax.broadcasted_iota(I32, (PAIR, PAIR), 0) // HEAD_DIM
    lj = lax.broadcasted_iota(I32, (PAIR, PAIR), 1) // HEAD_DIM
    same_head = li == lj

    def transform(it, carry):
        dirs = []
        for gi in range(GDN_GROUP):
            ci = it * GDN_GROUP + gi
            r0 = pl.multiple_of(ci * c, c)
            q = l2n(conv_silu(xq_ref, 0, r0, ci)) * (HEAD_DIM ** -0.5)
            k = l2n(conv_silu(xk_ref, 1, r0, ci))
            v = conv_silu(xv_ref, 2, r0, ci)
            ab = ab_ref[pl.ds(r0, c), :]
            gb = jnp.where(lane8 < 4, neg_a * jax.nn.softplus(ab + dtb), jax.nn.sigmoid(ab))
            g3 = _split3(gb)
            for d in range(2):
                dirs.append(dict(ci=ci, r0=r0, d=d, q=q, k=k, v=v, gb=gb, g3=g3))
        for dd in dirs:
            dd["gc"] = sum(_dot(tri_incl[dd["d"]], part) for part in dd["g3"])
        def stack(x):
            return jnp.concatenate([jnp.where(head_mask[0], x, 0.0), jnp.where(head_mask[1], x, 0.0)],
                                   axis=0).astype(BF16)

        for dd in dirs:
            d, gc, gb, k = dd["d"], dd["gc"], dd["gb"], dd["k"]
            c0 = 2 * d
            gct = gc.T
            g_last = gc[c - 1:c, :] if d == 0 else gc[0:1, :]
            gc_e = jnp.where(head_mask[0], gc[:, c0:c0 + 1], gc[:, c0 + 1:c0 + 2])
            beta_e = jnp.where(head_mask[0], gb[:, 4 + c0:5 + c0], gb[:, 5 + c0:6 + c0])
            gl_e = jnp.where(head_mask[0], g_last[:, c0:c0 + 1], g_last[:, c0 + 1:c0 + 2])
            gc_r = jnp.concatenate([gct[c0:c0 + 1, :], gct[c0 + 1:c0 + 2, :]], axis=1)
            dd["decay"] = jnp.where(incl2[d], jnp.exp(jnp.minimum(gc_e - gc_r, 0.0)), 0.0)
            kb = k * beta_e
            e_gc = jnp.exp(gc_e)
            dd.update(kb=kb.astype(BF16), kst=stack(k), e_gc=e_gc, erem=jnp.exp(gl_e - gc_e),
                      elast=jnp.exp(gl_e),
                      x0=jnp.concatenate([stack(dd["v"] * beta_e), stack(kb * e_gc)], axis=1))
        for dd in dirs:
            dd["a"] = _dot_nt(dd["kb"], dd["kst"])
        for dd in dirs:
            dd["qkr"] = _dot_nt(dd["q"].astype(BF16), dd["kst"])
        for dd in dirs:
            lmat = jnp.where(strict2[dd["d"]], dd["a"] * dd["decay"], 0.0)
            dd["pw"] = lmat.astype(BF16)
            dd["bd"] = stack(lmat)
            dd["t"] = eye2 - lmat
            dd["qk"] = jnp.where(incl2[dd["d"]], dd["qkr"] * dd["decay"], 0.0).astype(BF16)
        for dd in dirs:
            dd["sq"] = _dot(dd["pw"], dd["bd"])
        for dd in dirs:
            dd["pw"] = dd["sq"].astype(BF16)
            dd["bd"] = stack(dd["sq"])
        for _ in range(4):
            for dd in dirs:
                dd["tp"] = _dot(dd["t"].astype(BF16), dd["bd"])
            for dd in dirs:
                dd["sq"] = _dot(dd["pw"], dd["bd"])
            for dd in dirs:
                dd["t"] = dd["t"] + dd["tp"]
                dd["pw"] = dd["sq"].astype(BF16)
                dd["bd"] = stack(dd["sq"])
        for dd in dirs:
            dd["tp"] = _dot(dd["t"].astype(BF16), dd["bd"])
        for dd in dirs:
            dd["xu"] = _dot((dd["t"] + dd["tp"]).astype(BF16), dd["x0"])
        for dd in dirs:
            d, r0, ci = dd["d"], dd["r0"], dd["ci"]
            u_sum, w_sum = dd["xu"][:, :PAIR], dd["xu"][:, PAIR:]
            kd = (dd["k"] * dd["erem"]).astype(BF16)
            m_s[d, ci] = jnp.where(same_head, _dot_tn(kd, w_sum.astype(BF16)), 0.0).astype(BF16)
            n_s[d, ci] = jnp.where(same_head, _dot_tn(kd, u_sum.astype(BF16)), 0.0)
            qp_s[d, pl.ds(r0, c), :] = (dd["q"] * dd["e_gc"] - _dot(dd["qk"], stack(w_sum))).astype(BF16)
            op_s[d, pl.ds(r0, c), :] = _dot(dd["qk"], stack(u_sum))
            el_s[d, pl.ds(ci, 1), :] = dd["elast"]
        return carry

    lax.fori_loop(0, nch // GDN_GROUP, transform, 0)

    def scan(i, states):
        cis = (i, nch - 1 - i)
        rows = [pl.ds(pl.multiple_of(ci * c, c), c) for ci in cis]
        stb = [st.astype(BF16) for st in states]
        trans = [_dot(m_s[d, cis[d]], stb[d]) for d in range(2)]
        outs = [_dot(qp_s[d, rows[d], :], stb[d]) for d in range(2)]
        new_states = []
        for d in range(2):
            op_s[d, rows[d], :] += outs[d]
            new_states.append(states[d] * el_s[d, pl.ds(cis[d], 1), :] - trans[d] + n_s[d, cis[d]])
        return tuple(new_states)

    zero = jnp.zeros((PAIR, PAIR), F32)
    lax.fori_loop(0, nch, scan, (zero, zero))

    avg = avg_ref[...]
    tile = 512

    def finish(ti_, carry):
        r0 = pl.multiple_of(ti_ * tile, tile)
        o = op_s[0, pl.ds(r0, tile), :] + op_s[1, pl.ds(r0, tile), :]
        o = o * lax.rsqrt(_group_mean(o * o, avg) + NORM_EPS) * on_ref[...]
        gate = gate_ref[pl.ds(r0, tile), :]
        o_ref[pl.ds(r0, tile), :] = (o * (gate * jax.nn.sigmoid(gate))).astype(BF16)
        return carry

    lax.fori_loop(0, seq // tile, finish, 0)


def _gdn(pc, batch, seq, conv_w, a_log, dt_bias, o_norm_g):
    w = GDN_WIDTH
    pairs = w // PAIR
    pc3 = pc.reshape(batch, seq, pc.shape[-1])
    cw = conv_w.reshape(GDN_CONV, 3, w).transpose(1, 0, 2)
    cw = jnp.pad(cw, ((0, 0), (0, SUBLANES - GDN_CONV), (0, 0)))

    def pair_lanes(p):
        x = p.reshape(2, pairs, 2).transpose(1, 0, 2).reshape(pairs, 1, 4)
        return jnp.pad(x, ((0, 0), (0, 0), (0, LANES - 4)))

    nblk = w // PAIR

    def col(offset):
        return pl.BlockSpec((None, seq, PAIR), lambda b, hp: (b, 0, offset + hp))

    nch = seq // GDN_CHUNK
    return pl.pallas_call(
        _gdn_kernel,
        grid=(batch, pairs),
        in_specs=[col(0), col(nblk), col(2 * nblk), col(3 * nblk), col(4 * nblk),
                  pl.BlockSpec((3, SUBLANES, PAIR), lambda b, hp: (0, 0, hp)),
                  pl.BlockSpec((None, 1, LANES), lambda b, hp: (hp, 0, 0)),
                  pl.BlockSpec((None, 1, LANES), lambda b, hp: (hp, 0, 0)),
                  pl.BlockSpec((1, PAIR), lambda b, hp: (0, 0)),
                  pl.BlockSpec((PAIR, PAIR), lambda b, hp: (0, 0))],
        out_specs=pl.BlockSpec((None, seq, PAIR), lambda b, hp: (b, 0, hp)),
        out_shape=jax.ShapeDtypeStruct((batch, seq, w), BF16),
        scratch_shapes=[pltpu.VMEM((2, nch, PAIR, PAIR), BF16),
                        pltpu.VMEM((2, nch, PAIR, PAIR), F32),
                        pltpu.VMEM((2, seq, PAIR), BF16),
                        pltpu.VMEM((2, seq, PAIR), F32),
                        pltpu.VMEM((2, nch, PAIR), F32)],
        compiler_params=_cparams("parallel", "parallel"),
        name="gdn",
    )(pc3, pc3, pc3, pc3, pc3, cw, pair_lanes(a_log), pair_lanes(dt_bias),
      jnp.tile(o_norm_g, 2).reshape(1, PAIR),
      _block_avg_matrix(PAIR, HEAD_DIM)).reshape(batch * seq, w)


def _outproj_kernel(x_ref, ya_ref, yb_ref, yc_ref, wa_ref, wb_ref, wc_ref, g_ref, x1_ref, h_ref):
    x1 = (x_ref[...] + _dot(ya_ref[...], wa_ref[...]) + _dot(yb_ref[...], wb_ref[...])
          + _dot(yc_ref[...], wc_ref[...]))
    x1_ref[...] = x1
    h_ref[...] = (_rms(x1) * g_ref[...]).astype(BF16)


def _outproj(x2d, ya, yb, yc, w_out, g_ffn, tm=512):
    m, d = x2d.shape
    wa = w_out[:GMLP_WIDTH]
    wb = w_out[GMLP_WIDTH:GMLP_WIDTH + DSA_WIDTH]
    wc = w_out[GMLP_WIDTH + DSA_WIDTH:]
    const = lambda i: (0, 0)
    row = lambda wd: pl.BlockSpec((tm, wd), lambda i: (i, 0))
    return pl.pallas_call(
        _outproj_kernel,
        grid=(m // tm,),
        in_specs=[row(d), row(GMLP_WIDTH), row(DSA_WIDTH), row(GDN_WIDTH),
                  pl.BlockSpec(wa.shape, const), pl.BlockSpec(wb.shape, const),
                  pl.BlockSpec(wc.shape, const), pl.BlockSpec((1, d), const)],
        out_specs=[row(d), row(d)],
        out_shape=[jax.ShapeDtypeStruct((m, d), F32), jax.ShapeDtypeStruct((m, d), BF16)],
        compiler_params=_cparams("parallel"),
        name="outproj",
    )(x2d, ya, yb, yc, wa, wb, wc, g_ffn.reshape(1, d))


def _router_kernel(h_ref, wr_ref, tri_ref, pos_ref, aff_ref, offs_ref, bits_s, *, cap):
    seq = h_ref.shape[0]
    nblk = seq // LANES
    logits = _dot_nt(wr_ref[...], h_ref[...])
    ex = jnp.exp(logits - jnp.max(logits, axis=0, keepdims=True))
    aff = ex / jnp.sum(ex, axis=0, keepdims=True)
    aff_ref[...] = aff
    bits = lax.bitcast_convert_type(aff, I32)
    bits_s[...] = bits

    def bisect(it, prefix):
        cand = prefix | jnp.left_shift(jnp.int32(1), 30 - it)
        cnt = jnp.sum(jnp.where(bits_s[...] >= cand, 1.0, 0.0), axis=1, keepdims=True)
        return jnp.where(cnt >= cap, cand, prefix)

    thr = lax.fori_loop(0, 31, bisect, jnp.zeros((N_EXPERTS, 1), I32))
    n_gt = jnp.sum(jnp.where(bits > thr, 1.0, 0.0), axis=1, keepdims=True)
    need = cap - n_gt
    tri = tri_ref[...]
    lane = lax.broadcasted_iota(I32, (N_EXPERTS, LANES), 1)

    off_eq = jnp.zeros((N_EXPERTS, 1), F32)
    off_sel = jnp.zeros((N_EXPERTS, 1), F32)
    offs = jnp.zeros((N_EXPERTS, LANES), F32)
    for j in range(nblk):
        cols = slice(j * LANES, (j + 1) * LANES)
        bj = bits_s[:, cols]
        eq = bj == thr
        eq_f = jnp.where(eq, 1.0, 0.0)
        rank = off_eq + _dot(eq_f.astype(BF16), tri)
        sel = (bj > thr) | (eq & (rank < need))
        sel_f = jnp.where(sel, 1.0, 0.0)
        pos = off_sel + _dot(sel_f.astype(BF16), tri)
        pos_ref[:, cols] = jnp.where(sel, pos, -1.0)
        offs = jnp.where(lane == j, off_sel, offs)
        off_eq = off_eq + jnp.sum(eq_f, axis=1, keepdims=True)
        off_sel = off_sel + jnp.sum(sel_f, axis=1, keepdims=True)
    offs_ref[...] = jnp.where(lane == nblk, off_sel, offs).astype(I32)


def _router(h3, w_router):
    batch, seq, d = h3.shape
    cap = EC_CAPACITY * seq // N_EXPERTS
    i = jnp.arange(LANES)
    tri = jnp.where(i[:, None] < i[None, :], 1.0, 0.0).astype(BF16)
    out3 = lambda wd: pl.BlockSpec((None, N_EXPERTS, wd), lambda b: (b, 0, 0))
    return pl.pallas_call(
        functools.partial(_router_kernel, cap=cap),
        grid=(batch,),
        in_specs=[pl.BlockSpec((None, seq, d), lambda b: (b, 0, 0)),
                  pl.BlockSpec((N_EXPERTS, d), lambda b: (0, 0)),
                  pl.BlockSpec((LANES, LANES), lambda b: (0, 0))],
        out_specs=[out3(seq), out3(seq), out3(LANES)],
        out_shape=[jax.ShapeDtypeStruct((batch, N_EXPERTS, seq), F32),
                   jax.ShapeDtypeStruct((batch, N_EXPERTS, seq), F32),
                   jax.ShapeDtypeStruct((batch, N_EXPERTS, LANES), I32)],
        scratch_shapes=[pltpu.VMEM((N_EXPERTS, seq), I32)],
        compiler_params=_cparams("parallel"),
        name="router",
    )(h3, w_router.T.astype(BF16), tri)


def _moe_ffn_kernel(offs_ref, h_ref, pos_ref, aff_ref, wg_ref, wu_ref, wd_ref, ye_ref, x_s, gate_s, w_s, *, cap):
    nblk, tblk = pos_ref.shape
    per = tblk // LANES
    e = pl.program_id(0)
    b = pl.program_id(1)
    base = (b * N_EXPERTS + e) * OFFS_W

    @pl.when(b == 0)
    def _():
        w_s[0] = wg_ref[...].astype(BF16)
        w_s[1] = wu_ref[...].astype(BF16)
        w_s[2] = wd_ref[...].astype(BF16)

    rid = lax.broadcasted_iota(I32, (LANES, tblk), 0)

    def contribution(j, want):
        hit = pos_ref[pl.ds(j, 1), :] == want
        hj = h_ref[pl.ds(pl.multiple_of(j * tblk, tblk), tblk), :]
        return (_dot(jnp.where(hit, 1.0, 0.0).astype(BF16), hj),
                jnp.sum(jnp.where(hit, aff_ref[pl.ds(j, 1), :], 0.0), axis=1, keepdims=True))

    for c in range(cap // LANES):
        rows = slice(c * LANES, (c + 1) * LANES)
        first = jnp.int32(0)
        for j in range(nblk):
            first = first + (offs_ref[base + (j + 1) * per] <= c * LANES).astype(I32)
        x = jnp.zeros((LANES, h_ref.shape[1]), F32)
        gate = jnp.zeros((LANES, 1), F32)
        for k in range(MOE_STATIC):
            j = first + k
            shift = jnp.where(j < nblk, c * LANES, -cap)
            dx, dg = contribution(jnp.minimum(j, nblk - 1), (rid + shift).astype(F32))
            x, gate = x + dx, gate + dg
        x_s[rows, :] = x
        gate_s[rows, :] = jnp.broadcast_to(gate, (LANES, LANES))
        want = (rid + c * LANES).astype(F32)

        def rest(j, carry):
            @pl.when(offs_ref[base + j * per] < (c + 1) * LANES)
            def _():
                dx, dg = contribution(j, want)
                x_s[rows, :] += dx
                gate_s[rows, :] += dg

            return carry

        lax.fori_loop(first + MOE_STATIC, nblk, rest, 0)

    half = cap // 2
    for r in range(2):
        rows = slice(r * half, (r + 1) * half)
        x = x_s[rows, :].astype(BF16)
        g = _dot(x, w_s[0])
        u = _dot(x, w_s[1])
        hid = (g * jax.nn.sigmoid(g) * u).astype(BF16)
        ye_ref[rows, :] = (_dot(hid, w_s[2]) * gate_s[rows, 0:1]).astype(BF16)


def _moe_ffn(offs_flat, h3, pos4, aff4, wg, wu, wd, layer):
    batch, seq, d = h3.shape
    cap = EC_CAPACITY * seq // N_EXPERTS
    nblk, tblk = pos4.shape[2:]
    row = pl.BlockSpec((None, None, nblk, tblk), lambda e, b, o: (b, e, 0, 0))
    wspec = pl.BlockSpec((None, None, d, d), lambda e, b, o: (layer, e, 0, 0))
    return pl.pallas_call(
        functools.partial(_moe_ffn_kernel, cap=cap),
        grid_spec=pltpu.PrefetchScalarGridSpec(
            num_scalar_prefetch=1,
            grid=(N_EXPERTS, batch),
            in_specs=[pl.BlockSpec((None, seq, d), lambda e, b, o: (b, 0, 0)), row, row,
                      wspec, wspec, wspec],
            out_specs=pl.BlockSpec((None, None, cap, d), lambda e, b, o: (b, e, 0, 0)),
            scratch_shapes=[pltpu.VMEM((cap, d), F32), pltpu.VMEM((cap, LANES), F32),
                            pltpu.VMEM((3, d, d), BF16)]),
        out_shape=jax.ShapeDtypeStruct((batch, N_EXPERTS, cap, d), BF16),
        compiler_params=_cparams("arbitrary", "arbitrary"),
        name="moe_ffn",
    )(offs_flat, h3, pos4, aff4, wg, wu, wd)


def _moe_combine_kernel(offs_ref, x1_ref, ye_ref, ptok_ref, p_ref, wple_ref, wpg_ref, gple_ref, gpg_ref,
                        o_ref, acc_s, *, cap):
    tb = x1_ref.shape[0]
    b = pl.program_id(0)
    jb = pl.program_id(1)
    nsub = tb // LANES
    pack = 2 * SUBLANES

    def offsets(sub, e):
        at = (b * N_EXPERTS + e) * OFFS_W + jb * nsub + sub
        return offs_ref[at], offs_ref[at + 1]

    def scatter(win, align, group):
        cid = lax.broadcasted_iota(I32, (LANES, win), 1)
        for sub in range(nsub):
            rows = slice(sub * LANES, (sub + 1) * LANES)
            acc = x1_ref[rows, :]
            for e0 in range(0, N_EXPERTS, group):
                hits, wins = [], []
                for e in range(e0, e0 + group):
                    lo, _ = offsets(sub, e)
                    start = pl.multiple_of(jnp.minimum(_floor_to(lo, align), cap - win), align)
                    hit = ptok_ref[rows, e:e + 1] == (cid + start).astype(F32)
                    hits.append(jnp.where(hit, 1.0, 0.0).astype(BF16))
                    wins.append(ye_ref[e, pl.ds(start, win), :])
                acc = acc + _dot(jnp.concatenate(hits, axis=1), jnp.concatenate(wins, axis=0))
            acc_s[rows, :] = acc

    narrow = jnp.bool_(True)
    for sub in range(nsub):
        for e in range(N_EXPERTS):
            lo, hi = offsets(sub, e)
            narrow = narrow & (hi - _floor_to(lo, pack) <= LANES)

    @pl.when(narrow)
    def _():
        scatter(LANES, pack, 2)

    @pl.when(jnp.logical_not(narrow))
    def _():
        scatter(2 * LANES, LANES, 1)

    x2 = acc_s[...]
    emb = _rms(_dot(p_ref[...].astype(BF16), wple_ref[...])) * gple_ref[...]
    gate = jax.nn.sigmoid(_dot((_rms(x2) * gpg_ref[...]).astype(BF16), wpg_ref[...]))
    o_ref[...] = x2 + emb * gate


def _moe_combine(offs_flat, x1_3, ye, ptok, p4, layer, w_ple, w_pg, g_ple, g_pg, tb=512):
    batch, seq, d = x1_3.shape
    cap = ye.shape[2]
    pd = p4.shape[-1]
    const = lambda b, j, o: (0, 0)
    row = lambda wd: pl.BlockSpec((None, tb, wd), lambda b, j, o: (b, j, 0))
    return pl.pallas_call(
        functools.partial(_moe_combine_kernel, cap=cap),
        grid_spec=pltpu.PrefetchScalarGridSpec(
            num_scalar_prefetch=1,
            grid=(batch, seq // tb),
            in_specs=[row(d),
                      pl.BlockSpec((None, N_EXPERTS, cap, d), lambda b, j, o: (b, 0, 0, 0)),
                      row(N_EXPERTS),
                      pl.BlockSpec((None, None, tb, pd), lambda b, j, o: (layer, b, j, 0)),
                      pl.BlockSpec((pd, d), const), pl.BlockSpec((d, d), const),
                      pl.BlockSpec((1, d), const), pl.BlockSpec((1, d), const)],
            out_specs=row(d),
            scratch_shapes=[pltpu.VMEM((tb, d), F32)]),
        out_shape=jax.ShapeDtypeStruct((batch, seq, d), F32),
        compiler_params=_cparams("arbitrary", "arbitrary"),
        name="moe_combine_ple",
    )(offs_flat, x1_3, ye, ptok, p4, w_ple, w_pg, g_ple.reshape(1, d), g_pg.reshape(1, d))


def _layer(x2d, p4, tables, batch, seq, prm):
    d = x2d.shape[-1]
    pa, pb, pc = _inproj(x2d, prm["g_mix"].reshape(1, d), prm["w_in"],
                         (2 * GMLP_WIDTH, 3 * DSA_WIDTH, prm["w_in"].shape[1] - 2 * GMLP_WIDTH - 3 * DSA_WIDTH))
    ya = _gmlp(pa, prm["ln_v_g"], prm["ln_v_b"], prm["w_s"], prm["b_s"])
    yb = _dsa(_dsa_prep(pb, tables, batch, seq, prm["q_norm_g"], prm["k_norm_g"]), batch, seq)
    yc = _gdn(pc, batch, seq, prm["conv_w"], prm["a_log"], prm["dt_bias"], prm["o_norm_g"])
    x1, h = _outproj(x2d, ya, yb, yc, prm["w_out"], prm["g_ffn"])
    h3 = h.reshape(batch, seq, d)
    pos, aff, offs = _router(h3, prm["w_router"])
    offs_flat = offs[:, :, :OFFS_W].reshape(-1)
    nblk = seq // MOE_TBLK
    ye = _moe_ffn(offs_flat, h3, pos.reshape(batch, N_EXPERTS, nblk, MOE_TBLK),
                  aff.reshape(batch, N_EXPERTS, nblk, MOE_TBLK),
                  prm["w_e_gate"], prm["w_e_up"], prm["w_e_down"], prm["layer"])
    x2 = _moe_combine(offs_flat, x1.reshape(batch, seq, d), ye, pos.transpose(0, 2, 1), p4, prm["layer"],
                      prm["w_ple"], prm["w_ple_gate"], prm["g_ple"], prm["g_ple_gate"])
    return x2.reshape(batch * seq, d)


def _arrange_in_weights(w_in):
    heads = GDN_WIDTH // HEAD_DIM
    pairs = GDN_WIDTH // PAIR
    first = w_in.shape[-1] - 4 * heads
    cols = [first + which * 2 * heads + d * heads + 2 * hp + hh
            for hp in range(pairs) for which in range(2) for d in range(2) for hh in range(2)]
    gates = w_in[:, :, jnp.array(cols)].reshape(w_in.shape[:2] + (pairs, 8))
    gates = jnp.pad(gates, ((0, 0), (0, 0), (0, 0), (0, LANES - 8))).reshape(w_in.shape[:2] + (pairs * LANES,))
    return jnp.concatenate([w_in[:, :, :first], gates], axis=-1).astype(BF16)


def kernel(x, p, positions, g_mix, w_in, ln_v_g, ln_v_b, w_s, b_s, q_norm_g, k_norm_g, conv_w, a_log, dt_bias,
           o_norm_g, w_out, g_ffn, w_router, w_e_gate, w_e_up, w_e_down, w_ple, g_ple, g_ple_gate, w_ple_gate):
    batch, seq, d = x.shape
    depth = p.shape[0]
    w_in_b = _arrange_in_weights(w_in)
    tables = _rope_tables(positions)
    x2d = x.reshape(batch * seq, d)
    for i in range(depth):
        prm = dict(g_mix=g_mix[i], w_in=w_in_b[i], ln_v_g=ln_v_g[i], ln_v_b=ln_v_b[i], w_s=w_s[i], b_s=b_s[i],
                   q_norm_g=q_norm_g[i], k_norm_g=k_norm_g[i], conv_w=conv_w[i], a_log=a_log[i],
                   dt_bias=dt_bias[i], o_norm_g=o_norm_g[i], w_out=w_out[i].astype(BF16), g_ffn=g_ffn[i],
                   w_router=w_router[i], w_e_gate=w_e_gate, w_e_up=w_e_up, w_e_down=w_e_down, layer=i,
                   w_ple=w_ple[i].astype(BF16), g_ple=g_ple[i],
                   g_ple_gate=g_ple_gate[i], w_ple_gate=w_ple_gate[i].astype(BF16))
        x2d = _layer(x2d, p, tables, batch, seq, prm)
    return x2d.reshape(batch, seq, d)
```
